```python
import jax, jax.numpy as jnp
from jax import lax
import numpy as np

D_MODEL = 4096
BATCH = 4
SEQ = 2048
DEPTH = 2
DEC_BATCH = 8
DEC_SEQ = 1
PAST_LEN = 16384
PAGE_SIZE = 128

HEAD_DIM = 128
D_MIX = D_MODEL
D_FOX = D_MIX // 2
D_GMLP = D_MIX - D_FOX
N_FOX_HEADS = D_FOX // HEAD_DIM
N_GMLP_HEADS = 16
GMLP_HEAD_CH = D_GMLP // N_GMLP_HEADS
CHUNK = 128
Q_BLOCK = 128
N_EXPERTS = 32
TOP_K = 4
D_EXPERT = D_MODEL // 2
SWIGLU_LIMIT = 7.0
SWIGLU_ALPHA = 1.702
MOE_BLOCK = 128
NORM_EPS = 1e-6
FORGET_BIAS = 3.0
IN_COLS = 3 * D_FOX + N_FOX_HEADS + 2 * D_GMLP

kernel_name = 'hymba_fox_gmlp_moe_adaln_step'


def rmsnorm(x, g):
    xf = x.astype(jnp.float32)
    y = xf * lax.rsqrt(jnp.mean(xf * xf, axis=-1, keepdims=True) + NORM_EPS)
    return (y * g.astype(jnp.float32)).astype(x.dtype)


def layernorm(x, g, b):
    xf = x.astype(jnp.float32)
    mu = jnp.mean(xf, axis=-1, keepdims=True)
    var = jnp.mean(jnp.square(xf - mu), axis=-1, keepdims=True)
    y = (xf - mu) * lax.rsqrt(var + NORM_EPS)
    return (y * g.astype(jnp.float32) + b.astype(jnp.float32)).astype(x.dtype)


def ada_modulation(c, ada_w, ada_b):
    m = jax.nn.silu(c) @ ada_w + ada_b
    return jnp.split(m[:, None, :], 6, axis=-1)


def modulate(x, g, shift, scale):
    return rmsnorm(x, g) * (1 + scale) + shift


def mixer_projections(h, w_in, forget_b, ln_g, ln_b):
    B, L, _ = h.shape
    z = h @ w_in
    o1 = D_FOX
    o2 = 2 * D_FOX
    o3 = 3 * D_FOX
    o4 = o3 + N_FOX_HEADS
    o5 = o4 + D_GMLP
    q = z[..., :o1].reshape(B, L, N_FOX_HEADS, HEAD_DIM)
    k = z[..., o1:o2].reshape(B, L, N_FOX_HEADS, HEAD_DIM)
    v = z[..., o2:o3].reshape(B, L, N_FOX_HEADS, HEAD_DIM)
    logf = jax.nn.log_sigmoid((z[..., o3:o4] + forget_b).astype(jnp.float32))
    gu = jax.nn.gelu(z[..., o4:o5], approximate=False)
    gv = layernorm(jax.nn.gelu(z[..., o5:], approximate=False), ln_g, ln_b)
    return q, k, v, logf, gu, gv


def chunk_gmlp(gu, gv, ws, bs):
    B, L, _ = gv.shape
    n = -(-L // CHUNK)
    pad = n * CHUNK - L
    vp = jnp.pad(gv, ((0, 0), (0, pad), (0, 0))).reshape(B, n, CHUNK, N_GMLP_HEADS, GMLP_HEAD_CH)
    w = ws * jnp.tril(jnp.ones((CHUNK, CHUNK), ws.dtype))
    s = jnp.einsum('hts,bnshc->bnthc', w, vp) + bs.T[None, None, :, :, None]
    s = s.reshape(B, n * CHUNK, D_GMLP)[:, :L]
    return gu * s


def fox_prompt(q, k, v, logf):
    B, T, H, Dh = q.shape
    nb = T // Q_BLOCK
    scale = HEAD_DIM ** -0.5
    Ct = jnp.cumsum(logf, axis=1).transpose(0, 2, 1)
    k_pos = jnp.arange(T)

    def block(args):
        qb, cb, i = args
        s = jnp.einsum('bqhd,bkhd->bhqk', qb, k, preferred_element_type=jnp.float32) * scale
        s = s + (cb[..., :, None] - Ct[..., None, :])
        q_pos = i * Q_BLOCK + jnp.arange(Q_BLOCK)
        s = jnp.where((k_pos[None, :] <= q_pos[:, None])[None, None], s, -jnp.inf)
        p = jax.nn.softmax(s, axis=-1).astype(v.dtype)
        return jnp.einsum('bhqk,bkhd->bqhd', p, v)

    qs = q.reshape(B, nb, Q_BLOCK, H, Dh).transpose(1, 0, 2, 3, 4)
    cs = Ct.reshape(B, H, nb, Q_BLOCK).transpose(2, 0, 1, 3)
    o = lax.map(block, (qs, cs, jnp.arange(nb)))
    return o.transpose(1, 0, 2, 3, 4).reshape(B, T, H * Dh)


def fox_sample(q, k, v, logf, k_past, v_past, logf_past):
    B, S, H, Dh = q.shape
    P = k_past.shape[1]
    scale = HEAD_DIM ** -0.5
    C = jnp.cumsum(jnp.concatenate([logf_past.astype(jnp.float32), logf], axis=1), axis=1)
    Cp = C[:, :P].transpose(0, 2, 1)
    Cn = C[:, P:].transpose(0, 2, 1)
    s_past = jnp.einsum('bqhd,bkhd->bhqk', q, k_past, preferred_element_type=jnp.float32) * scale
    s_past = s_past + (Cn[..., :, None] - Cp[..., None, :])
    s_new = jnp.einsum('bqhd,bkhd->bhqk', q, k, preferred_element_type=jnp.float32) * scale
    s_new = s_new + (Cn[..., :, None] - Cn[..., None, :])
    s_new = jnp.where(jnp.tril(jnp.ones((S, S), bool))[None, None], s_new, -jnp.inf)
    p = jax.nn.softmax(jnp.concatenate([s_past, s_new], axis=-1), axis=-1).astype(v.dtype)
    o = jnp.einsum('bhqk,bkhd->bqhd', p[..., :P], v_past) + jnp.einsum('bhqk,bkhd->bqhd', p[..., P:], v)
    return o.reshape(B, S, H * Dh)


def merge_groups(o_fox, o_gmlp, out_g, w_out):
    o = jnp.concatenate([rmsnorm(o_fox, out_g[:D_FOX]), rmsnorm(o_gmlp, out_g[D_FOX:])], axis=-1)
    return o @ w_out


def moe(h, l, router_w, router_b, w_gate_up, b_gate_up, w_down, b_down):
    T = h.shape[0]
    logits = jnp.matmul(h, router_w[l], preferred_element_type=jnp.float32) + router_b[l].astype(jnp.float32)
    top_v, top_e = lax.top_k(logits, TOP_K)
    gates = jax.nn.softmax(top_v, axis=-1)
    n_assign = T * TOP_K
    blk = int(min(MOE_BLOCK, max(8, n_assign // N_EXPERTS)))
    n_blocks = -(-n_assign // blk) + N_EXPERTS
    flat_e = top_e.reshape(n_assign).astype(jnp.int32)
    flat_t = jnp.arange(n_assign, dtype=jnp.int32) // TOP_K
    flat_g = gates.reshape(n_assign)
    order = jnp.argsort(flat_e)
    sorted_e = flat_e[order]
    counts = jnp.bincount(flat_e, length=N_EXPERTS).astype(jnp.int32)
    padded = (counts + blk - 1) // blk * blk
    pad_end = jnp.cumsum(padded)
    dest = (pad_end - padded)[sorted_e] + jnp.arange(n_assign, dtype=jnp.int32) - (jnp.cumsum(counts) - counts)[sorted_e]
    row_tok = jnp.full((n_blocks * blk,), T, jnp.int32).at[dest].set(flat_t[order])
    row_gate = jnp.zeros((n_blocks * blk,), jnp.float32).at[dest].set(flat_g[order])
    block_e = jnp.minimum(jnp.searchsorted(pad_end, jnp.arange(n_blocks, dtype=jnp.int32) * blk, side='right'), N_EXPERTS - 1)
    h_pad = jnp.concatenate([h, jnp.zeros((1, h.shape[1]), h.dtype)], axis=0)

    def expert_block(args):
        rows, e = args
        xb = h_pad[rows]
        gu = xb @ w_gate_up[l, e] + b_gate_up[l, e]
        g = jnp.minimum(gu[:, :D_EXPERT], SWIGLU_LIMIT)
        u = jnp.clip(gu[:, D_EXPERT:], -SWIGLU_LIMIT, SWIGLU_LIMIT)
        a = g * jax.nn.sigmoid(SWIGLU_ALPHA * g) * (u + 1)
        return a @ w_down[l, e] + b_down[l, e]

    out_rows = lax.map(expert_block, (row_tok.reshape(n_blocks, blk), block_e)).reshape(n_blocks * blk, -1)
    out = jax.ops.segment_sum(out_rows * row_gate[:, None].astype(out_rows.dtype), row_tok, num_segments=T + 1)
    return out[:T]


def setup_inputs(seed: int = 0) -> dict:
    key = jax.random.key(seed)
    ks = jax.random.split(key, 32)
    n_pages = PAST_LEN // PAGE_SIZE
    n_pool = (DEC_BATCH * n_pages * 5) // 4

    def nrm(k, shape, s):
        return jax.random.normal(k, shape, jnp.float32) * s

    page_table = jax.random.permutation(ks[7], n_pool)[:DEC_BATCH * n_pages].reshape(DEC_BATCH, n_pages).astype(jnp.int32)
    return {
        'x_prompt': nrm(ks[0], (BATCH, SEQ, D_MODEL), 1.0),
        'x_sample': nrm(ks[1], (DEC_BATCH, DEC_SEQ, D_MODEL), 1.0),
        'c_prompt': nrm(ks[2], (BATCH, D_MODEL), 1.0),
        'c_sample': nrm(ks[3], (DEC_BATCH, D_MODEL), 1.0),
        'cache_k': nrm(ks[4], (DEPTH, n_pool, PAGE_SIZE, N_FOX_HEADS, HEAD_DIM), 1.0),
        'cache_v': nrm(ks[5], (DEPTH, n_pool, PAGE_SIZE, N_FOX_HEADS, HEAD_DIM), 1.0),
        'cache_logf': jax.nn.log_sigmoid(FORGET_BIAS + nrm(ks[6], (DEPTH, n_pool, PAGE_SIZE, N_FOX_HEADS), 1.0)),
        'page_table': page_table,
        'ada_w': nrm(ks[8], (DEPTH, D_MODEL, 6 * D_MODEL), 0.5 * D_MODEL ** -0.5),
        'ada_b': nrm(ks[9], (DEPTH, 6 * D_MODEL), 0.02),
        'norm1_g': 1.0 + nrm(ks[10], (DEPTH, D_MODEL), 0.1),
        'norm2_g': 1.0 + nrm(ks[11], (DEPTH, D_MODEL), 0.1),
        'w_in': nrm(ks[12], (DEPTH, D_MODEL, IN_COLS), D_MODEL ** -0.5),
        'forget_b': FORGET_BIAS + nrm(ks[13], (DEPTH, N_FOX_HEADS), 0.1),
        'gmlp_ln_g': 1.0 + nrm(ks[14], (DEPTH, D_GMLP), 0.1),
        'gmlp_ln_b': nrm(ks[15], (DEPTH, D_GMLP), 0.02),
        'gmlp_ws': nrm(ks[16], (DEPTH, N_GMLP_HEADS, CHUNK, CHUNK), CHUNK ** -0.5),
        'gmlp_bs': 1.0 + nrm(ks[17], (DEPTH, N_GMLP_HEADS, CHUNK), 0.1),
        'out_g': 1.0 + nrm(ks[18], (DEPTH, D_MIX), 0.1),
        'w_out': nrm(ks[19], (DEPTH, D_MIX, D_MODEL), D_MIX ** -0.5),
        'router_w': nrm(ks[20], (DEPTH, D_MODEL, N_EXPERTS), D_MODEL ** -0.5),
        'router_b': nrm(ks[21], (DEPTH, N_EXPERTS), 0.01),
        'w_gate_up': nrm(ks[22], (DEPTH, N_EXPERTS, D_MODEL, 2 * D_EXPERT), D_MODEL ** -0.5),
        'b_gate_up': nrm(ks[23], (DEPTH, N_EXPERTS, 2 * D_EXPERT), 0.01),
        'w_down': nrm(ks[24], (DEPTH, N_EXPERTS, D_EXPERT, D_MODEL), D_EXPERT ** -0.5),
        'b_down': nrm(ks[25], (DEPTH, N_EXPERTS, D_MODEL), 0.01),
        'final_g': 1.0 + nrm(ks[26], (D_MODEL,), 0.1),
    }


def reference(x_prompt, x_sample, c_prompt, c_sample, cache_k, cache_v, cache_logf, page_table,
              ada_w, ada_b, norm1_g, norm2_g, w_in, forget_b, gmlp_ln_g, gmlp_ln_b, gmlp_ws, gmlp_bs,
              out_g, w_out, router_w, router_b, w_gate_up, b_gate_up, w_down, b_down, final_g):
    xp, xs = x_prompt, x_sample
    n_db = x_sample.shape[0]
    kp_l, vp_l, fp_l, ks_l, vs_l, fs_l, gs_l = [], [], [], [], [], [], []
    for l in range(DEPTH):
        sh1p, sc1p, g1p, sh2p, sc2p, g2p = ada_modulation(c_prompt, ada_w[l], ada_b[l])
        sh1s, sc1s, g1s, sh2s, sc2s, g2s = ada_modulation(c_sample, ada_w[l], ada_b[l])
        q, k, v, lf, gu, gv = mixer_projections(modulate(xp, norm1_g[l], sh1p, sc1p), w_in[l], forget_b[l], gmlp_ln_g[l], gmlp_ln_b[l])
        o = merge_groups(fox_prompt(q, k, v, lf), chunk_gmlp(gu, gv, gmlp_ws[l], gmlp_bs[l]), out_g[l], w_out[l])
        xp = xp + g1p * o
        kp_l.append(k)
        vp_l.append(v)
        fp_l.append(lf)
        q, k, v, lf, gu, gv = mixer_projections(modulate(xs, norm1_g[l], sh1s, sc1s), w_in[l], forget_b[l], gmlp_ln_g[l], gmlp_ln_b[l])
        k_past = cache_k[l, page_table].reshape(n_db, -1, N_FOX_HEADS, HEAD_DIM)
        v_past = cache_v[l, page_table].reshape(n_db, -1, N_FOX_HEADS, HEAD_DIM)
        lf_past = cache_logf[l, page_table].reshape(n_db, -1, N_FOX_HEADS)
        o = merge_groups(fox_sample(q, k, v, lf, k_past, v_past, lf_past), chunk_gmlp(gu, gv, gmlp_ws[l], gmlp_bs[l]), out_g[l], w_out[l])
        xs = xs + g1s * o
        ks_l.append(k)
        vs_l.append(v)
        fs_l.append(lf)
        gs_l.append(gv)
        hp = modulate(xp, norm2_g[l], sh2p, sc2p).reshape(-1, D_MODEL)
        xp = xp + g2p * moe(hp, l, router_w, router_b, w_gate_up, b_gate_up, w_down, b_down).reshape(xp.shape)
        hs = modulate(xs, norm2_g[l], sh2s, sc2s).reshape(-1, D_MODEL)
        xs = xs + g2s * moe(hs, l, router_w, router_b, w_gate_up, b_gate_up, w_down, b_down).reshape(xs.shape)
    y_prompt = rmsnorm(xp, final_g)
    y_sample = rmsnorm(xs, final_g)
    k_prompt = jnp.stack(kp_l)
    v_prompt = jnp.stack(vp_l)
    logf_prompt = jnp.stack(fp_l)
    k_sample = jnp.stack(ks_l)
    v_sample = jnp.stack(vs_l)
    logf_sample = jnp.stack(fs_l)
    gmlp_v_sample = jnp.stack(gs_l)
    return (y_prompt, y_sample, k_prompt, v_prompt, logf_prompt, k_sample, v_sample, logf_sample, gmlp_v_sample)
```

```python
import functools

import jax
import jax.numpy as jnp
from jax import lax
from jax.experimental import pallas as pl
from jax.experimental.pallas import tpu as pltpu

F32 = jnp.float32
BF16 = jnp.bfloat16
I32 = jnp.int32

NORM_EPS = 1e-6
TOP_K = 4
SWIGLU_LIMIT = 7.0
SWIGLU_ALPHA = 1.702
LANES = 128
ROW_PAD = 16
NEG_BIG = -1e30
MIB = 1024 * 1024


def _tile(dim, pref):
    return pref if dim % pref == 0 else dim


def _row_tile(M, pref, rows_per_mod=None):
    if rows_per_mod in (None, 1):
        return _tile(M, pref)
    assert M % rows_per_mod == 0
    return _tile(rows_per_mod, pref)


def _params(sem, vmem_mib):
    return pltpu.CompilerParams(dimension_semantics=sem, vmem_limit_bytes=vmem_mib * MIB)


def _gelu(x):
    return 0.5 * x * (1.0 + lax.erf(x * 0.7071067811865476))


def _rms(x):
    return x * lax.rsqrt(jnp.mean(x * x, axis=-1, keepdims=True) + NORM_EPS)


def _ada_kernel(c_ref, w_ref, b_ref, o_ref):
    c = c_ref[...]
    a = (c * jax.nn.sigmoid(c)).astype(BF16)
    o_ref[...] = jnp.dot(a, w_ref[...].astype(BF16), preferred_element_type=F32) + b_ref[...]


def _ada(c_all, ada_w, ada_b):
    L, D, N = ada_w.shape
    R = c_all.shape[0]
    tn = _tile(N, 512)
    return pl.pallas_call(
        _ada_kernel,
        grid=(L, N // tn),
        in_specs=[
            pl.BlockSpec((R, D), lambda l, j: (0, 0)),
            pl.BlockSpec((None, D, tn), lambda l, j: (l, 0, j)),
            pl.BlockSpec((None, 1, tn), lambda l, j: (l, 0, j)),
        ],
        out_specs=pl.BlockSpec((None, R, tn), lambda l, j: (l, 0, j)),
        out_shape=jax.ShapeDtypeStruct((L, R, N), F32),
        compiler_params=_params(("arbitrary", "arbitrary"), 40),
        name="ada",
    )(c_all, ada_w, ada_b.reshape(L, 1, N))


def _modnorm_kernel(x_ref, g_ref, sh_ref, sc_ref, h_ref):
    h = (_rms(x_ref[...]) * g_ref[...]) * (1.0 + sc_ref[...]) + sh_ref[...]
    h_ref[...] = h.astype(h_ref.dtype)


def _router_kernel(x_ref, g_ref, sh_ref, sc_ref, rw_ref, rb_ref, h_ref, e_ref, p_ref):
    h = (_rms(x_ref[...]) * g_ref[...]) * (1.0 + sc_ref[...]) + sh_ref[...]
    h_ref[...] = h
    logits = jnp.dot(h, rw_ref[...], precision=lax.Precision.HIGHEST, preferred_element_type=F32) + rb_ref[...]
    lane = lax.broadcasted_iota(I32, logits.shape, 1).astype(F32)
    cur = logits
    vals, idxs = [], []
    for _ in range(TOP_K):
        m = jnp.max(cur, axis=-1, keepdims=True)
        idx = jnp.min(jnp.where(cur == m, lane, float(LANES)), axis=-1, keepdims=True)
        vals.append(m)
        idxs.append(idx)
        cur = jnp.where(lane == idx, -jnp.inf, cur)
    exps = [jnp.exp(v - vals[0]) for v in vals]
    denom = exps[0]
    for e in exps[1:]:
        denom = denom + e
    e_out = jnp.zeros(logits.shape, F32)
    p_out = jnp.zeros(logits.shape, F32)
    for k in range(TOP_K):
        e_out = jnp.where(lane == float(k), idxs[k], e_out)
        p_out = jnp.where(lane == float(k), exps[k] / denom, p_out)
    e_ref[...] = e_out.astype(I32)
    p_ref[...] = p_out


def _mod_block(tm, D, rows_per_mod):
    if rows_per_mod == 1:
        return (tm, D), lambda i: (i, 0)
    return (None, 1, D), lambda i: ((i * tm) // rows_per_mod, 0, 0)


def _mod_specs(n_rows, tm, D, rows_per_mod):
    shape, index = _mod_block(tm, D, rows_per_mod)
    return pl.BlockSpec(shape, index)


def _modnorm(x, g, shift, scale, rows_per_mod, out_dtype):
    M, D = x.shape
    tm = _row_tile(M, 256, rows_per_mod)
    mod = _mod_specs(M, tm, D, rows_per_mod)
    return pl.pallas_call(
        _modnorm_kernel,
        grid=(M // tm,),
        in_specs=[pl.BlockSpec((tm, D), lambda i: (i, 0)), pl.BlockSpec((1, D), lambda i: (0, 0)), mod, mod],
        out_specs=pl.BlockSpec((tm, D), lambda i: (i, 0)),
        out_shape=jax.ShapeDtypeStruct((M, D), out_dtype),
        compiler_params=_params(("arbitrary",), 40),
        name="modnorm",
    )(x, g.reshape(1, D), shift, scale)


def _modnorm_router(x, g, shift, scale, rows_per_mod, rw_pad, rb_pad):
    M, D = x.shape
    tm = _row_tile(M, 256, rows_per_mod)
    mod = _mod_specs(M, tm, D, rows_per_mod)
    return pl.pallas_call(
        _router_kernel,
        grid=(M // tm,),
        in_specs=[pl.BlockSpec((tm, D), lambda i: (i, 0)), pl.BlockSpec((1, D), lambda i: (0, 0)), mod, mod,
                  pl.BlockSpec((D, LANES), lambda i: (0, 0)), pl.BlockSpec((1, LANES), lambda i: (0, 0))],
        out_specs=[pl.BlockSpec((tm, D), lambda i: (i, 0)), pl.BlockSpec((tm, LANES), lambda i: (i, 0)),
                   pl.BlockSpec((tm, LANES), lambda i: (i, 0))],
        out_shape=[jax.ShapeDtypeStruct((M, D), F32), jax.ShapeDtypeStruct((M, LANES), I32),
                   jax.ShapeDtypeStruct((M, LANES), F32)],
        compiler_params=_params(("arbitrary",), 48),
        name="modnorm_router",
    )(x, g.reshape(1, D), shift, scale, rw_pad, rb_pad)


def _rmsnorm_kernel(x_ref, g_ref, o_ref):
    o_ref[...] = _rms(x_ref[...]) * g_ref[...]


def _rmsnorm(x, g):
    M, D = x.shape
    tm = _tile(M, 256)
    return pl.pallas_call(
        _rmsnorm_kernel,
        grid=(M // tm,),
        in_specs=[pl.BlockSpec((tm, D), lambda i: (i, 0)), pl.BlockSpec((1, D), lambda i: (0, 0))],
        out_specs=pl.BlockSpec((tm, D), lambda i: (i, 0)),
        out_shape=jax.ShapeDtypeStruct((M, D), F32),
        compiler_params=_params(("arbitrary",), 40),
        name="final_rmsnorm",
    )(x, g.reshape(1, D))


def _mm_kernel(x_ref, w_ref, *rest, mode):
    wb_ref = rest[-1]

    @pl.when(pl.program_id(1) == 0)
    def _():
        wb_ref[...] = w_ref[...].astype(BF16)

    acc = jnp.dot(x_ref[...], wb_ref[...], preferred_element_type=F32)
    if mode == "f32":
        rest[0][...] = acc
    elif mode == "bf16":
        rest[0][...] = acc.astype(BF16)
    elif mode == "both":
        rest[0][...] = acc
        rest[1][...] = acc.astype(BF16)
    elif mode == "logf":
        fb_ref, o_ref = rest[0], rest[1]
        o_ref[...] = jax.nn.log_sigmoid(acc + fb_ref[...])
    elif mode == "resid":
        xr_ref, gate_ref, o_ref = rest[0], rest[1], rest[2]
        o_ref[...] = xr_ref[...] + gate_ref[...] * acc
    else:
        raise ValueError(mode)


def _matmul(x, w, w_lead, col_off, n_cols, mode, extras=(), rows_per_mod=None, tn_pref=512):
    M, K = x.shape
    tm = _row_tile(M, 1024, rows_per_mod)
    tn = _tile(n_cols, tn_pref)
    assert col_off % tn == 0
    joff = col_off // tn
    ni, nj = M // tm, n_cols // tn
    lead = tuple(w_lead)
    w_spec = pl.BlockSpec((None,) * len(lead) + (K, tn), lambda j, i: lead + (0, joff + j))
    in_specs = [pl.BlockSpec((tm, K), lambda j, i: (i, 0)), w_spec]
    o_spec = pl.BlockSpec((tm, tn), lambda j, i: (i, j))
    if mode == "f32":
        out_specs, out_shape = [o_spec], [jax.ShapeDtypeStruct((M, n_cols), F32)]
    elif mode == "bf16":
        out_specs, out_shape = [o_spec], [jax.ShapeDtypeStruct((M, n_cols), BF16)]
    elif mode == "both":
        out_specs = [o_spec, o_spec]
        out_shape = [jax.ShapeDtypeStruct((M, n_cols), F32), jax.ShapeDtypeStruct((M, n_cols), BF16)]
    elif mode == "logf":
        in_specs.append(pl.BlockSpec((1, tn), lambda j, i: (0, j)))
        out_specs, out_shape = [o_spec], [jax.ShapeDtypeStruct((M, n_cols), F32)]
    elif mode == "resid":
        in_specs.append(o_spec)
        if rows_per_mod == 1:
            in_specs.append(pl.BlockSpec((tm, tn), lambda j, i: (i, j)))
        else:
            in_specs.append(pl.BlockSpec((None, 1, tn), lambda j, i: ((i * tm) // rows_per_mod, 0, j)))
        out_specs, out_shape = [o_spec], [jax.ShapeDtypeStruct((M, n_cols), F32)]
    outs = pl.pallas_call(
        functools.partial(_mm_kernel, mode=mode),
        grid=(nj, ni),
        in_specs=in_specs,
        out_specs=out_specs,
        out_shape=out_shape,
        scratch_shapes=[pltpu.VMEM((K, tn), BF16)],
        compiler_params=_params(("arbitrary", "arbitrary"), 56),
        name="matmul_" + mode,
    )(x, w, *extras)
    return outs if len(outs) > 1 else outs[0]


def _cumsum_kernel(lf_ref, o_ref, *, H):
    x = lf_ref[...].T[:H]
    T = x.shape[1]
    lane = lax.broadcasted_iota(I32, x.shape, 1)
    d = 1
    while d < T:
        x = x + jnp.where(lane >= d, pltpu.roll(x, d, axis=1), 0.0)
        d *= 2
    o_ref[...] = x


def _cumsum_t(logf_pad, B, T, H):
    return pl.pallas_call(
        functools.partial(_cumsum_kernel, H=H),
        grid=(B,),
        in_specs=[pl.BlockSpec((T, LANES), lambda b: (b, 0))],
        out_specs=pl.BlockSpec((None, H, T), lambda b: (b, 0, 0)),
        out_shape=jax.ShapeDtypeStruct((B, H, T), F32),
        compiler_params=_params(("arbitrary",), 32),
        name="logf_cumsum",
    )(logf_pad)


def _fox_kernel(q_ref, k_ref, v_ref, c_ref, o_ref, *, tq, scale):
    qi = pl.program_id(2)
    q = q_ref[...]
    Dh = q.shape[1]

    def block(j, carry, masked):
        m, l, acc = carry
        k0 = pl.multiple_of(j * tq, tq)
        kb = k_ref[pl.ds(k0, tq), :]
        vb = v_ref[pl.ds(k0, tq), :]
        s = lax.dot_general(q, kb, (((1,), (1,)), ((), ())), preferred_element_type=F32) * scale
        s = s - c_ref[:, pl.ds(k0, tq)]
        if masked:
            row = lax.broadcasted_iota(I32, s.shape, 0)
            col = lax.broadcasted_iota(I32, s.shape, 1)
            s = jnp.where(col <= row, s, -jnp.inf)
        m_new = jnp.maximum(m, jnp.max(s, axis=-1, keepdims=True))
        alpha = jnp.exp(m - m_new)
        p = jnp.exp(s - m_new)
        l = alpha * l + jnp.sum(p, axis=-1, keepdims=True)
        acc = alpha * acc + jnp.dot(p.astype(BF16), vb, preferred_element_type=F32)
        return m_new, l, acc

    init = (jnp.full((tq, 1), -jnp.inf, F32), jnp.zeros((tq, 1), F32), jnp.zeros((tq, Dh), F32))
    carry = lax.fori_loop(0, qi, lambda j, c: block(j, c, False), init)
    m, l, acc = block(qi, carry, True)
    o_ref[...] = acc / l


def _fox_prompt(q, k, v, ct, B, T, H, Dh):
    tq = _tile(T, 256)
    nq = T // tq
    return pl.pallas_call(
        functools.partial(_fox_kernel, tq=tq, scale=Dh ** -0.5),
        grid=(B, H, nq),
        in_specs=[
            pl.BlockSpec((tq, Dh), lambda b, h, i: (b * nq + i, h)),
            pl.BlockSpec((T, Dh), lambda b, h, i: (b, h)),
            pl.BlockSpec((T, Dh), lambda b, h, i: (b, h)),
            pl.BlockSpec((None, 1, T), lambda b, h, i: (b * H + h, 0, 0)),
        ],
        out_specs=pl.BlockSpec((tq, Dh), lambda b, h, i: (b * nq + i, h)),
        out_shape=jax.ShapeDtypeStruct((B * T, H * Dh), F32),
        compiler_params=_params(("arbitrary", "arbitrary", "arbitrary"), 32),
        name="fox_prompt",
    )(q, k, v, ct.reshape(B * H, 1, T))


def _decode_kernel(pt_ref, q_ref, kn_ref, vn_ref, lfn_ref, ck_ref, cv_ref, clf_ref, o_ref,
                   s_ref, c_ref, p_ref, carry_ref, acc_ref, pn_ref, *, NP, PS, H, scale):
    ph = pl.program_id(1)
    p = pl.program_id(2)
    off = pl.multiple_of(p * PS, PS)

    @pl.when(ph == 0)
    def _():
        @pl.when(p == 0)
        def _():
            carry_ref[...] = jnp.zeros_like(carry_ref)

        r0 = lax.broadcasted_iota(I32, (PS, PS), 0)
        r1 = lax.broadcasted_iota(I32, (PS, PS), 1)
        upper = (r0 <= r1).astype(F32)
        cpage = lax.dot_general(clf_ref[...], upper, (((0,), (0,)), ((), ())),
                                precision=lax.Precision.HIGHEST, preferred_element_type=F32)
        cpage = cpage + carry_ref[...]
        carry_ref[...] = cpage[:, PS - 1:PS]
        c_ref[:, pl.ds(off, PS)] = cpage

        qb = q_ref[...].astype(BF16)
        row = lax.broadcasted_iota(I32, (H, PS), 0)
        acc = jnp.zeros((H, PS), F32)
        for h in range(H):
            kh = ck_ref[:, h, :].astype(BF16)
            s = lax.dot_general(qb, kh, (((1,), (1,)), ((), ())), preferred_element_type=F32)
            acc = acc + jnp.where(row == h, s, 0.0)
        s_ref[:, pl.ds(off, PS)] = acc * scale

    @pl.when(ph == 1)
    def _():
        @pl.when(p == 0)
        def _():
            cn = carry_ref[...] + lfn_ref[...]
            logits = s_ref[...] + (cn - c_ref[...])
            s_new = jnp.sum(q_ref[...] * kn_ref[...], axis=-1, keepdims=True) * scale
            m = jnp.maximum(jnp.max(logits, axis=-1, keepdims=True), s_new)
            e = jnp.exp(logits - m)
            en = jnp.exp(s_new - m)
            denom = jnp.sum(e, axis=-1, keepdims=True) + en
            p_ref[...] = (e / denom).astype(BF16)
            pn_ref[...] = en / denom
            acc_ref[...] = jnp.zeros_like(acc_ref)

        pb = p_ref[:, pl.ds(off, PS)]
        acc = acc_ref[...]
        row = lax.broadcasted_iota(I32, acc.shape, 0)
        for h in range(H):
            vh = cv_ref[:, h, :].astype(BF16)
            o = jnp.dot(pb, vh, preferred_element_type=F32)
            acc = acc + jnp.where(row == h, o, 0.0)
        acc_ref[...] = acc

        @pl.when(p == NP - 1)
        def _():
            o_ref[...] = acc + pn_ref[...] * vn_ref[...]


def _fox_decode(q, kn, vn, lfn, cache_k, cache_v, cache_logf, page_table, l):
    Bs, H, Dh = q.shape
    NP = page_table.shape[1]
    PS = cache_k.shape[2]
    P = NP * PS

    def kidx(b, ph, p, pt):
        return (l, pt[b * NP + jnp.where(ph == 0, p, NP - 1)], 0, 0, 0)

    def vidx(b, ph, p, pt):
        return (l, pt[b * NP + jnp.where(ph == 0, 0, p)], 0, 0, 0)

    def fidx(b, ph, p, pt):
        return (l, pt[b * NP + jnp.where(ph == 0, p, NP - 1)], 0, 0)

    tok = lambda b, ph, p, pt: (b, 0, 0)
    grid_spec = pltpu.PrefetchScalarGridSpec(
        num_scalar_prefetch=1,
        grid=(Bs, 2, NP),
        in_specs=[
            pl.BlockSpec((None, H, Dh), tok), pl.BlockSpec((None, H, Dh), tok), pl.BlockSpec((None, H, Dh), tok),
            pl.BlockSpec((None, H, 1), tok),
            pl.BlockSpec((None, None, PS, H, Dh), kidx),
            pl.BlockSpec((None, None, PS, H, Dh), vidx),
            pl.BlockSpec((None, None, PS, H), fidx),
        ],
        out_specs=pl.BlockSpec((None, H, Dh), tok),
        scratch_shapes=[pltpu.VMEM((H, P), F32), pltpu.VMEM((H, P), F32), pltpu.VMEM((H, P), BF16),
                        pltpu.VMEM((H, 1), F32), pltpu.VMEM((H, Dh), F32), pltpu.VMEM((H, 1), F32)],
    )
    return pl.pallas_call(
        functools.partial(_decode_kernel, NP=NP, PS=PS, H=H, scale=Dh ** -0.5),
        grid_spec=grid_spec,
        out_shape=jax.ShapeDtypeStruct((Bs, H, Dh), F32),
        compiler_params=_params(("arbitrary", "arbitrary", "arbitrary"), 32),
        name="fox_decode",
    )(page_table.reshape(-1), q, kn, vn, lfn, cache_k, cache_v, cache_logf)


def _gmlp_norm_v(zv, g, b):
    gv = _gelu(zv)
    mu = jnp.mean(gv, axis=-1, keepdims=True)
    var = jnp.mean(jnp.square(gv - mu), axis=-1, keepdims=True)
    return (gv - mu) * lax.rsqrt(var + NORM_EPS) * g + b


def _gmlp_kernel(zu_ref, zv_ref, g_ref, b_ref, ws_ref, bst_ref, o_ref, *, Hg, CW):
    v = _gmlp_norm_v(zv_ref[...], g_ref[...], b_ref[...])
    C = v.shape[0]
    tril = lax.broadcasted_iota(I32, (C, C), 0) >= lax.broadcasted_iota(I32, (C, C), 1)
    for h in range(Hg):
        sl = slice(h * CW, (h + 1) * CW)
        w = jnp.where(tril, ws_ref[h], 0.0).astype(BF16)
        s = jnp.dot(w, v[:, sl].astype(BF16), preferred_element_type=F32) + bst_ref[:, h:h + 1]
        o_ref[:, sl] = _gelu(zu_ref[:, sl]) * s


def _gmlp_prompt(zg, ln_g, ln_b, ws, bs):
    M, DG2 = zg.shape
    DG = DG2 // 2
    Hg, C, _ = ws.shape
    CW = DG // Hg
    return pl.pallas_call(
        functools.partial(_gmlp_kernel, Hg=Hg, CW=CW),
        grid=(M // C,),
        in_specs=[
            pl.BlockSpec((C, DG), lambda i: (i, 0)), pl.BlockSpec((C, DG), lambda i: (i, 1)),
            pl.BlockSpec((1, DG), lambda i: (0, 0)), pl.BlockSpec((1, DG), lambda i: (0, 0)),
            pl.BlockSpec((Hg, C, C), lambda i: (0, 0, 0)), pl.BlockSpec((C, Hg), lambda i: (0, 0)),
        ],
        out_specs=pl.BlockSpec((C, DG), lambda i: (i, 0)),
        out_shape=jax.ShapeDtypeStruct((M, DG), F32),
        compiler_params=_params(("arbitrary",), 32),
        name="gmlp_prompt",
    )(zg, zg, ln_g.reshape(1, DG), ln_b.reshape(1, DG), ws, bs.T)


def _gmlp_sample_kernel(zu_ref, zv_ref, g_ref, b_ref, w0_ref, b0_ref, v_ref, o_ref):
    v = _gmlp_norm_v(zv_ref[...], g_ref[...], b_ref[...])
    v_ref[...] = v
    o_ref[...] = _gelu(zu_ref[...]) * (w0_ref[...] * v + b0_ref[...])


def _gmlp_sample(zg, ln_g, ln_b, ws, bs):
    R, DG2 = zg.shape
    DG = DG2 // 2
    Hg = ws.shape[0]
    CW = DG // Hg
    w0 = jnp.repeat(ws[:, 0, 0], CW).reshape(1, DG)
    b0 = jnp.repeat(bs[:, 0], CW).reshape(1, DG)
    row = pl.BlockSpec((1, DG), lambda i: (0, 0))
    return pl.pallas_call(
        _gmlp_sample_kernel,
        grid=(1,),
        in_specs=[pl.BlockSpec((R, DG), lambda i: (0, 0)), pl.BlockSpec((R, DG), lambda i: (0, 1)), row, row, row, row],
        out_specs=[pl.BlockSpec((R, DG), lambda i: (0, 0)), pl.BlockSpec((R, DG), lambda i: (0, 0))],
        out_shape=[jax.ShapeDtypeStruct((R, DG), F32), jax.ShapeDtypeStruct((R, DG), F32)],
        name="gmlp_sample",
    )(zg, zg, ln_g.reshape(1, DG), ln_b.reshape(1, DG), w0, b0)


def _merge_kernel(a_ref, b_ref, ga_ref, gb_ref, o_ref):
    DA = a_ref.shape[1]
    o_ref[:, :DA] = (_rms(a_ref[...]) * ga_ref[...]).astype(BF16)
    o_ref[:, DA:] = (_rms(b_ref[...]) * gb_ref[...]).astype(BF16)


def _merge_norm(a, b, g):
    M, DA = a.shape
    DB = b.shape[1]
    tm = _tile(M, 256)
    return pl.pallas_call(
        _merge_kernel,
        grid=(M // tm,),
        in_specs=[pl.BlockSpec((tm, DA), lambda i: (i, 0)), pl.BlockSpec((tm, DB), lambda i: (i, 0)),
                  pl.BlockSpec((1, DA), lambda i: (0, 0)), pl.BlockSpec((1, DB), lambda i: (0, 0))],
        out_specs=pl.BlockSpec((tm, DA + DB), lambda i: (i, 0)),
        out_shape=jax.ShapeDtypeStruct((M, DA + DB), BF16),
        compiler_params=_params(("arbitrary",), 32),
        name="merge_norm",
    )(a, b, g[:DA].reshape(1, DA), g[DA:].reshape(1, DB))


def _dispatch_kernel(nbu_ref, tok_ref, hp_ref, hs_ref, o_ref, buf_ref, sem, *, tm, M):
    b = pl.program_id(0)

    @pl.when(b < nbu_ref[0])
    def _():
        def issue(r, c):
            t = tok_ref[b * tm + r]

            @pl.when(t < M)
            def _():
                pltpu.make_async_copy(hp_ref.at[pl.ds(t, 1)], buf_ref.at[pl.ds(r, 1)], sem).start()

            @pl.when(t >= M)
            def _():
                pltpu.make_async_copy(hs_ref.at[pl.ds(t - M, 1)], buf_ref.at[pl.ds(r, 1)], sem).start()

            return c

        lax.fori_loop(0, tm, issue, 0)

        def wait(r, c):
            pltpu.make_async_copy(hp_ref.at[pl.ds(0, 1)], buf_ref.at[pl.ds(r, 1)], sem).wait()
            return c

        lax.fori_loop(0, tm, wait, 0)
        o_ref[...] = buf_ref[...].astype(BF16)

    @pl.when(b >= nbu_ref[0])
    def _():
        o_ref[...] = jnp.zeros_like(o_ref)


def _dispatch(nbu, row_tok, hp, hs, tm):
    M, D = hp.shape
    R = row_tok.shape[0]
    nb = R // tm
    grid_spec = pltpu.PrefetchScalarGridSpec(
        num_scalar_prefetch=2,
        grid=(nb,),
        in_specs=[pl.BlockSpec(memory_space=pl.ANY), pl.BlockSpec(memory_space=pl.ANY)],
        out_specs=pl.BlockSpec((tm, D), lambda b, nbu, tok: (b, 0)),
        scratch_shapes=[pltpu.VMEM((tm, D), F32), pltpu.SemaphoreType.DMA(())],
    )
    return pl.pallas_call(
        functools.partial(_dispatch_kernel, tm=tm, M=M),
        grid_spec=grid_spec,
        out_shape=jax.ShapeDtypeStruct((R, D), BF16),
        compiler_params=_params(("arbitrary",), 32),
        name="moe_dispatch",
    )(nbu, row_tok, hp, hs)


def _gate_up_kernel(nbu_ref, be_ref, first_ref, x_ref, wg_ref, wu_ref, bg_ref, bu_ref, a_ref, wgb_ref, wub_ref):
    b = pl.program_id(1)

    @pl.when(b < nbu_ref[0])
    def _():
        @pl.when(first_ref[b] == 1)
        def _():
            wgb_ref[...] = wg_ref[...].astype(BF16)
            wub_ref[...] = wu_ref[...].astype(BF16)

        x = x_ref[...]
        g = jnp.dot(x, wgb_ref[...], preferred_element_type=F32) + bg_ref[...]
        u = jnp.dot(x, wub_ref[...], preferred_element_type=F32) + bu_ref[...]
        g = jnp.minimum(g, SWIGLU_LIMIT)
        u = jnp.clip(u, -SWIGLU_LIMIT, SWIGLU_LIMIT)
        a_ref[...] = (g * jax.nn.sigmoid(SWIGLU_ALPHA * g) * (u + 1.0)).astype(BF16)

    @pl.when(b >= nbu_ref[0])
    def _():
        a_ref[...] = jnp.zeros_like(a_ref)


def _gate_up(nbu, block_e, first, xs, w_gate_up, b_gate_up, l, tm):
    R, D = xs.shape
    _, E, _, DE2 = w_gate_up.shape
    DE = DE2 // 2
    tn = _tile(DE, 512)
    nj = DE // tn
    nb = R // tm

    def eff(b, nbu):
        return jnp.minimum(b, nbu[0] - 1)

    grid_spec = pltpu.PrefetchScalarGridSpec(
        num_scalar_prefetch=3,
        grid=(nj, nb),
        in_specs=[
            pl.BlockSpec((tm, D), lambda j, b, nbu, be, fi: (eff(b, nbu), 0)),
            pl.BlockSpec((None, None, D, tn), lambda j, b, nbu, be, fi: (l, be[eff(b, nbu)], 0, j)),
            pl.BlockSpec((None, None, D, tn), lambda j, b, nbu, be, fi: (l, be[eff(b, nbu)], 0, nj + j)),
            pl.BlockSpec((None, None, 1, tn), lambda j, b, nbu, be, fi: (l, be[eff(b, nbu)], 0, j)),
            pl.BlockSpec((None, None, 1, tn), lambda j, b, nbu, be, fi: (l, be[eff(b, nbu)], 0, nj + j)),
        ],
        out_specs=pl.BlockSpec((tm, tn), lambda j, b, nbu, be, fi: (b, j)),
        scratch_shapes=[pltpu.VMEM((D, tn), BF16), pltpu.VMEM((D, tn), BF16)],
    )
    bgu = b_gate_up.reshape(b_gate_up.shape[0], E, 1, DE2)
    return pl.pallas_call(
        _gate_up_kernel,
        grid_spec=grid_spec,
        out_shape=jax.ShapeDtypeStruct((R, DE), BF16),
        compiler_params=_params(("arbitrary", "arbitrary"), 56),
        name="moe_gate_up",
    )(nbu, block_e, first, xs, w_gate_up, w_gate_up, bgu, bgu)


def _down_kernel(nbu_ref, be_ref, first_ref, a_ref, w_ref, bias_ref, o_ref, wb_ref):
    b = pl.program_id(1)

    @pl.when(b < nbu_ref[0])
    def _():
        @pl.when(first_ref[b] == 1)
        def _():
            wb_ref[...] = w_ref[...].astype(BF16)

        o_ref[...] = jnp.dot(a_ref[...], wb_ref[...], preferred_element_type=F32) + bias_ref[...]

    @pl.when(b >= nbu_ref[0])
    def _():
        o_ref[...] = jnp.zeros_like(o_ref)


def _down(nbu, block_e, first, a, w_down, b_down, l, tm):
    R, DE = a.shape
    _, E, _, D = w_down.shape
    tn = _tile(D, 1024)
    nj = D // tn
    nb = R // tm

    def eff(b, nbu):
        return jnp.minimum(b, nbu[0] - 1)

    grid_spec = pltpu.PrefetchScalarGridSpec(
        num_scalar_prefetch=3,
        grid=(nj, nb),
        in_specs=[
            pl.BlockSpec((tm, DE), lambda j, b, nbu, be, fi: (eff(b, nbu), 0)),
            pl.BlockSpec((None, None, DE, tn), lambda j, b, nbu, be, fi: (l, be[eff(b, nbu)], 0, j)),
            pl.BlockSpec((None, None, 1, tn), lambda j, b, nbu, be, fi: (l, be[eff(b, nbu)], 0, j)),
        ],
        out_specs=pl.BlockSpec((tm, tn), lambda j, b, nbu, be, fi: (b, j)),
        scratch_shapes=[pltpu.VMEM((DE, tn), BF16)],
    )
    return pl.pallas_call(
        _down_kernel,
        grid_spec=grid_spec,
        out_shape=jax.ShapeDtypeStruct((R, D), F32),
        compiler_params=_params(("arbitrary", "arbitrary"), 48),
        name="moe_down",
    )(nbu, block_e, first, a, w_down, b_down.reshape(b_down.shape[0], E, 1, D))


def _combine_kernel(pos_ref, rows_ref, p_ref, x_ref, gate_ref, o_ref, buf_ref, sem, *, tc):
    i = pl.program_id(0)
    n = tc * TOP_K

    def issue(a, c):
        r = a // TOP_K
        k = a % TOP_K
        src = pos_ref[i * n + a]
        pltpu.make_async_copy(rows_ref.at[pl.ds(src, 1)], buf_ref.at[k, pl.ds(r, 1)], sem).start()
        return c

    lax.fori_loop(0, n, issue, 0)

    def wait(a, c):
        pltpu.make_async_copy(rows_ref.at[pl.ds(0, 1)], buf_ref.at[0, pl.ds(0, 1)], sem).wait()
        return c

    lax.fori_loop(0, n, wait, 0)
    y = p_ref[:, 0:1] * buf_ref[0]
    for k in range(1, TOP_K):
        y = y + p_ref[:, k:k + 1] * buf_ref[k]
    o_ref[...] = x_ref[...] + gate_ref[...] * y


def _combine(pos, rows, probs, x, gate, rows_per_mod):
    M, D = x.shape
    tc = _row_tile(M, 128, rows_per_mod)
    gate_shape, gate_index = _mod_block(tc, D, rows_per_mod)
    grid_spec = pltpu.PrefetchScalarGridSpec(
        num_scalar_prefetch=1,
        grid=(M // tc,),
        in_specs=[
            pl.BlockSpec(memory_space=pl.ANY),
            pl.BlockSpec((tc, LANES), lambda i, pos: (i, 0)),
            pl.BlockSpec((tc, D), lambda i, pos: (i, 0)),
            pl.BlockSpec(gate_shape, lambda i, pos: gate_index(i)),
        ],
        out_specs=pl.BlockSpec((tc, D), lambda i, pos: (i, 0)),
        scratch_shapes=[pltpu.VMEM((TOP_K, tc, D), F32), pltpu.SemaphoreType.DMA(())],
    )
    return pl.pallas_call(
        functools.partial(_combine_kernel, tc=tc),
        grid_spec=grid_spec,
        out_shape=jax.ShapeDtypeStruct((M, D), F32),
        compiler_params=_params(("arbitrary",), 40),
        name="moe_combine",
    )(pos, rows, probs, x, gate)


def _route(top_e, E, tm):
    n_tok = top_e.shape[0]
    n_assign = n_tok * TOP_K
    flat_e = top_e.reshape(n_assign)
    onehot = (flat_e[:, None] == jnp.arange(E, dtype=I32)[None, :]).astype(I32)
    csum = jnp.cumsum(onehot, axis=0)
    rank = jnp.take_along_axis(csum, flat_e[:, None], axis=1)[:, 0] - 1
    counts = csum[-1]
    padded = (counts + tm - 1) // tm * tm
    pad_end = jnp.cumsum(padded)
    dest = (pad_end - padded)[flat_e] + rank
    nb = -(-n_assign // tm) + E
    row_tok = jnp.full((nb * tm,), n_tok, I32).at[dest].set(jnp.arange(n_assign, dtype=I32) // TOP_K)
    block_e = jnp.minimum(jnp.searchsorted(pad_end, jnp.arange(nb, dtype=I32) * tm, side="right"), E - 1).astype(I32)
    first = jnp.concatenate([jnp.ones((1,), I32), (block_e[1:] != block_e[:-1]).astype(I32)])
    nbu = (pad_end[-1] // tm).astype(I32).reshape(1)
    return row_tok, dest.astype(I32), block_e, first, nbu


def kernel(x_prompt, x_sample, c_prompt, c_sample, cache_k, cache_v, cache_logf, page_table, ada_w, ada_b, norm1_g, norm2_g, w_in, forget_b, gmlp_ln_g, gmlp_ln_b, gmlp_ws, gmlp_bs, out_g, w_out, router_w, router_b, w_gate_up, b_gate_up, w_down, b_down, final_g):
    B, T, D = x_prompt.shape
    Bs, Ss, _ = x_sample.shape
    assert Ss == 1, "the sample group carries one new token per sequence"
    L = ada_w.shape[0]
    H, Dh = cache_k.shape[3], cache_k.shape[4]
    DF = H * Dh
    DG = gmlp_ln_g.shape[1]
    E = router_w.shape[2]
    M = B * T
    SP = -(-Bs // ROW_PAD) * ROW_PAD
    assert (3 * DF) % LANES == 0 and H <= LANES and E <= LANES
    moe_tm = 256

    n_c = B + Bs
    RC = -(-n_c // ROW_PAD) * ROW_PAD
    c_all = jnp.concatenate([c_prompt, c_sample, jnp.zeros((RC - n_c, D), F32)], axis=0)
    mod = _ada(c_all, ada_w, ada_b)

    xp = x_prompt.reshape(M, D)
    xs = jnp.pad(x_sample.reshape(Bs, D), ((0, SP - Bs), (0, 0)))
    page_table = page_table.astype(I32)

    kp_l, vp_l, fp_l, ks_l, vs_l, fs_l, gs_l = [], [], [], [], [], [], []
    for l in range(L):
        mp = [m.reshape(B, 1, D) for m in jnp.split(mod[l, :B], 6, axis=-1)]
        ms = [jnp.pad(m, ((0, SP - Bs), (0, 0))) for m in jnp.split(mod[l, B:n_c], 6, axis=-1)]
        fb_pad = jnp.pad(forget_b[l], (0, LANES - H)).reshape(1, LANES)
        w_g = w_in[l, :, 3 * DF + H:]
        groups = (
            (xp, mp, T, True),
            (xs, ms, 1, False),
        )
        new_x = []
        for x, md, rpm, is_prompt in groups:
            sh1, sc1, g1 = md[0], md[1], md[2]
            h = _modnorm(x, norm1_g[l], sh1, sc1, rpm, BF16)
            if is_prompt:
                q_b = _matmul(h, w_in, (l,), 0, DF, "bf16")
                k_f, k_b = _matmul(h, w_in, (l,), DF, DF, "both")
                v_f, v_b = _matmul(h, w_in, (l,), 2 * DF, DF, "both")
            else:
                q_f = _matmul(h, w_in, (l,), 0, DF, "f32")
                k_f = _matmul(h, w_in, (l,), DF, DF, "f32")
                v_f = _matmul(h, w_in, (l,), 2 * DF, DF, "f32")
            logf_pad = _matmul(h, w_in, (l,), 3 * DF, LANES, "logf", extras=(fb_pad,))
            zg = _matmul(h, w_g, (), 0, 2 * DG, "f32")
            if is_prompt:
                ct = _cumsum_t(logf_pad, B, T, H)
                o_fox = _fox_prompt(q_b, k_b, v_b, ct, B, T, H, Dh)
                o_g = _gmlp_prompt(zg, gmlp_ln_g[l], gmlp_ln_b[l], gmlp_ws[l], gmlp_bs[l])
                kp_l.append(k_f.reshape(B, T, H, Dh))
                vp_l.append(v_f.reshape(B, T, H, Dh))
                fp_l.append(logf_pad[:, :H].reshape(B, T, H))
            else:
                lf_s = logf_pad[:Bs, :H]
                o_att = _fox_decode(q_f[:Bs].reshape(Bs, H, Dh), k_f[:Bs].reshape(Bs, H, Dh),
                                    v_f[:Bs].reshape(Bs, H, Dh), lf_s.reshape(Bs, H, 1),
                                    cache_k, cache_v, cache_logf, page_table, l)
                o_fox = jnp.pad(o_att.reshape(Bs, DF), ((0, SP - Bs), (0, 0)))
                gv, o_g = _gmlp_sample(zg, gmlp_ln_g[l], gmlp_ln_b[l], gmlp_ws[l], gmlp_bs[l])
                ks_l.append(k_f[:Bs].reshape(Bs, 1, H, Dh))
                vs_l.append(v_f[:Bs].reshape(Bs, 1, H, Dh))
                fs_l.append(lf_s.reshape(Bs, 1, H))
                gs_l.append(gv[:Bs].reshape(Bs, 1, DG))
            on = _merge_norm(o_fox, o_g, out_g[l])
            new_x.append(_matmul(on, w_out, (l,), 0, D, "resid", extras=(x, g1), rows_per_mod=rpm))
        xp, xs = new_x

        rw_pad = jnp.pad(router_w[l], ((0, 0), (0, LANES - E)))
        rb_pad = jnp.pad(router_b[l], (0, LANES - E), constant_values=NEG_BIG).reshape(1, LANES)
        h2p, e_p, p_p = _modnorm_router(xp, norm2_g[l], mp[3], mp[4], T, rw_pad, rb_pad)
        h2s, e_s, p_s = _modnorm_router(xs, norm2_g[l], ms[3], ms[4], 1, rw_pad, rb_pad)
        top_e = jnp.concatenate([e_p[:, :TOP_K], e_s[:Bs, :TOP_K]], axis=0)
        row_tok, dest, block_e, first, nbu = _route(top_e, E, moe_tm)
        xs_sorted = _dispatch(nbu, row_tok, h2p, h2s, moe_tm)
        act = _gate_up(nbu, block_e, first, xs_sorted, w_gate_up, b_gate_up, l, moe_tm)
        rows = _down(nbu, block_e, first, act, w_down, b_down, l, moe_tm)
        pos_p = dest[:M * TOP_K]
        pos_s = jnp.pad(dest[M * TOP_K:], (0, (SP - Bs) * TOP_K))
        p_s = jnp.where(jnp.arange(SP)[:, None] < Bs, p_s, 0.0)
        xp = _combine(pos_p, rows, p_p, xp, mp[5], T)
        xs = _combine(pos_s, rows, p_s, xs, ms[5], 1)

    y_prompt = _rmsnorm(xp, final_g).reshape(B, T, D)
    y_sample = _rmsnorm(xs, final_g)[:Bs].reshape(Bs, 1, D)
    return (y_prompt, y_sample, jnp.stack(kp_l), jnp.stack(vp_l), jnp.stack(fp_l),
            jnp.stack(ks_l), jnp.stack(vs_l), jnp.stack(fs_l), jnp.stack(gs_l))
```

```python
import functools

import jax
import jax.numpy as jnp
from jax import lax
from jax.experimental import pallas as pl
from jax.experimental.pallas import tpu as pltpu

F32 = jnp.float32
BF16 = jnp.bfloat16
I32 = jnp.int32

NORM_EPS = 1e-6
TOP_K = 4
SWIGLU_LIMIT = 7.0
SWIGLU_ALPHA = 1.702
LANES = 128
ROW_PAD = 16
NEG_BIG = -1e30
MIB = 1024 * 1024


def _tile(dim, pref):
    return pref if dim % pref == 0 else dim


def _row_tile(M, pref, rows_per_mod=None):
    if rows_per_mod in (None, 1):
        return _tile(M, pref)
    assert M % rows_per_mod == 0
    return _tile(rows_per_mod, pref)


def _params(sem, vmem_mib):
    return pltpu.CompilerParams(dimension_semantics=sem, vmem_limit_bytes=vmem_mib * MIB)


def _gelu(x):
    return 0.5 * x * (1.0 + lax.erf(x * 0.7071067811865476))


def _rms(x):
    return x * lax.rsqrt(jnp.mean(x * x, axis=-1, keepdims=True) + NORM_EPS)


def _ada_kernel(c_ref, w_ref, b_ref, o_ref):
    c = c_ref[...]
    a = (c * jax.nn.sigmoid(c)).astype(BF16)
    o_ref[...] = jnp.dot(a, w_ref[...].astype(BF16), preferred_element_type=F32) + b_ref[...]


def _ada(c_all, ada_w, ada_b):
    L, D, N = ada_w.shape
    R = c_all.shape[0]
    tn = _tile(N, 512)
    return pl.pallas_call(
        _ada_kernel,
        grid=(L, N // tn),
        in_specs=[
            pl.BlockSpec((R, D), lambda l, j: (0, 0)),
            pl.BlockSpec((None, D, tn), lambda l, j: (l, 0, j)),
            pl.BlockSpec((None, 1, tn), lambda l, j: (l, 0, j)),
        ],
        out_specs=pl.BlockSpec((None, R, tn), lambda l, j: (l, 0, j)),
        out_shape=jax.ShapeDtypeStruct((L, R, N), F32),
        compiler_params=_params(("arbitrary", "arbitrary"), 40),
        name="ada",
    )(c_all, ada_w, ada_b.reshape(L, 1, N))


def _modnorm_kernel(x_ref, g_ref, sh_ref, sc_ref, h_ref):
    h = (_rms(x_ref[...]) * g_ref[...]) * (1.0 + sc_ref[...]) + sh_ref[...]
    h_ref[...] = h.astype(h_ref.dtype)


def _router_kernel(x_ref, g_ref, sh_ref, sc_ref, rw_ref, rb_ref, c0_ref, *rest, n_valid, aliased):
    h_ref, e_ref, p_ref, r_ref, cnt_ref, carry_ref = rest[1:] if aliased else rest

    @pl.when(pl.program_id(0) == 0)
    def _():
        carry_ref[...] = c0_ref[...]

    h = (_rms(x_ref[...]) * g_ref[...]) * (1.0 + sc_ref[...]) + sh_ref[...]
    h_ref[...] = h
    logits = jnp.dot(h.astype(BF16), rw_ref[...].astype(BF16), preferred_element_type=F32) + rb_ref[...]
    tm = logits.shape[0]
    lane = lax.broadcasted_iota(I32, logits.shape, 1).astype(F32)
    cur = logits
    vals, idxs = [], []
    for _ in range(TOP_K):
        m = jnp.max(cur, axis=-1, keepdims=True)
        idx = jnp.min(jnp.where(cur == m, lane, float(LANES)), axis=-1, keepdims=True)
        vals.append(m)
        idxs.append(idx)
        cur = jnp.where(lane == idx, -jnp.inf, cur)
    exps = [jnp.exp(v - vals[0]) for v in vals]
    denom = exps[0]
    for e in exps[1:]:
        denom = denom + e

    valid = lax.broadcasted_iota(I32, logits.shape, 0) < n_valid
    memb = jnp.zeros(logits.shape, F32)
    for k in range(TOP_K):
        memb = jnp.where((lane == idxs[k]) & valid, 1.0, memb)
    earlier = lax.broadcasted_iota(I32, (tm, tm), 1) < lax.broadcasted_iota(I32, (tm, tm), 0)
    before = jnp.dot(jnp.where(earlier, 1.0, 0.0).astype(BF16), memb.astype(BF16),
                     preferred_element_type=F32) + carry_ref[...]
    total = carry_ref[...] + jnp.sum(memb, axis=0, keepdims=True)
    carry_ref[...] = total
    cnt_ref[...] = total

    e_out = jnp.zeros(logits.shape, F32)
    p_out = jnp.zeros(logits.shape, F32)
    r_out = jnp.zeros(logits.shape, F32)
    for k in range(TOP_K):
        rank = jnp.sum(jnp.where(lane == idxs[k], before, 0.0), axis=-1, keepdims=True)
        e_out = jnp.where(lane == float(k), idxs[k], e_out)
        p_out = jnp.where(lane == float(k), exps[k] / denom, p_out)
        r_out = jnp.where(lane == float(k), rank, r_out)
    e_ref[...] = e_out.astype(I32)
    p_ref[...] = p_out
    r_ref[...] = r_out.astype(I32)


def _mod_block(tm, D, rows_per_mod):
    if rows_per_mod == 1:
        return (tm, D), lambda i: (i, 0)
    return (None, 1, D), lambda i: ((i * tm) // rows_per_mod, 0, 0)


def _mod_specs(n_rows, tm, D, rows_per_mod):
    shape, index = _mod_block(tm, D, rows_per_mod)
    return pl.BlockSpec(shape, index)


def _modnorm(x, g, shift, scale, rows_per_mod, out_dtype):
    M, D = x.shape
    tm = _row_tile(M, 256, rows_per_mod)
    mod = _mod_specs(M, tm, D, rows_per_mod)
    return pl.pallas_call(
        _modnorm_kernel,
        grid=(M // tm,),
        in_specs=[pl.BlockSpec((tm, D), lambda i: (i, 0)), pl.BlockSpec((1, D), lambda i: (0, 0)), mod, mod],
        out_specs=pl.BlockSpec((tm, D), lambda i: (i, 0)),
        out_shape=jax.ShapeDtypeStruct((M, D), out_dtype),
        compiler_params=_params(("arbitrary",), 40),
        name="modnorm",
    )(x, g.reshape(1, D), shift, scale)


def _modnorm_router(x, g, shift, scale, rows_per_mod, rw_pad, rb_pad, counts0, n_valid, total_rows, h_all=None):
    M, D = x.shape
    tm = _row_tile(M, 256, rows_per_mod)
    mod = _mod_specs(M, tm, D, rows_per_mod)
    row_off = 0 if h_all is None else total_rows - M
    assert row_off % tm == 0
    boff = row_off // tm
    small = pl.BlockSpec((tm, LANES), lambda i: (i, 0))
    in_specs = [pl.BlockSpec((tm, D), lambda i: (i, 0)), pl.BlockSpec((1, D), lambda i: (0, 0)), mod, mod,
                pl.BlockSpec((D, LANES), lambda i: (0, 0)), pl.BlockSpec((1, LANES), lambda i: (0, 0)),
                pl.BlockSpec((1, LANES), lambda i: (0, 0))]
    args = [x, g.reshape(1, D), shift, scale, rw_pad, rb_pad, counts0]
    aliases = {}
    if h_all is not None:
        in_specs.append(pl.BlockSpec(memory_space=pl.ANY))
        args.append(h_all)
        aliases = {len(args) - 1: 0}
    return pl.pallas_call(
        functools.partial(_router_kernel, n_valid=n_valid, aliased=h_all is not None),
        grid=(M // tm,),
        in_specs=in_specs,
        out_specs=[pl.BlockSpec((tm, D), lambda i: (boff + i, 0)), small, small, small,
                   pl.BlockSpec((1, LANES), lambda i: (0, 0))],
        out_shape=[jax.ShapeDtypeStruct((total_rows, D), F32), jax.ShapeDtypeStruct((M, LANES), I32),
                   jax.ShapeDtypeStruct((M, LANES), F32), jax.ShapeDtypeStruct((M, LANES), I32),
                   jax.ShapeDtypeStruct((1, LANES), F32)],
        scratch_shapes=[pltpu.VMEM((1, LANES), F32)],
        input_output_aliases=aliases,
        compiler_params=_params(("arbitrary",), 48),
        name="modnorm_router",
    )(*args)


def _rmsnorm_kernel(x_ref, g_ref, o_ref):
    o_ref[...] = _rms(x_ref[...]) * g_ref[...]


def _rmsnorm(x, g):
    M, D = x.shape
    tm = _tile(M, 256)
    return pl.pallas_call(
        _rmsnorm_kernel,
        grid=(M // tm,),
        in_specs=[pl.BlockSpec((tm, D), lambda i: (i, 0)), pl.BlockSpec((1, D), lambda i: (0, 0))],
        out_specs=pl.BlockSpec((tm, D), lambda i: (i, 0)),
        out_shape=jax.ShapeDtypeStruct((M, D), F32),
        compiler_params=_params(("arbitrary",), 40),
        name="final_rmsnorm",
    )(x, g.reshape(1, D))


def _mm_kernel(x_ref, w_ref, *rest, mode, lane_shift, aliased):
    wb_ref = rest[-1]
    if lane_shift:
        wn_ref, rest = rest[0], rest[1:]
    if aliased:
        rest = rest[1:]

    @pl.when(pl.program_id(1) == 0)
    def _():
        if not lane_shift:
            wb_ref[...] = w_ref[...].astype(BF16)
        else:
            K, tn = w_ref.shape
            kc = _tile(K, 512)
            for kk in range(K // kc):
                rows = slice(kk * kc, (kk + 1) * kc)
                wide = jnp.concatenate([w_ref[rows, :], wn_ref[rows, :]], axis=1)
                wb_ref[rows, :] = pltpu.roll(wide, tn + LANES - lane_shift, axis=1)[:, :tn].astype(BF16)

    acc = jnp.dot(x_ref[...], wb_ref[...], preferred_element_type=F32)
    if mode == "f32":
        rest[0][...] = acc
    elif mode == "bf16":
        rest[0][...] = acc.astype(BF16)
    elif mode == "both":
        rest[0][...] = acc
        rest[1][...] = acc.astype(BF16)
    elif mode == "logf":
        fb_ref, o_ref = rest[0], rest[1]
        o_ref[...] = jax.nn.log_sigmoid(acc + fb_ref[...])
    elif mode == "resid":
        xr_ref, gate_ref, o_ref = rest[0], rest[1], rest[2]
        o_ref[...] = xr_ref[...] + gate_ref[...] * acc
    else:
        raise ValueError(mode)


def _matmul(x, w, w_lead, col_off, n_cols, mode, extras=(), rows_per_mod=None, tn_pref=512, lane_shift=0, stack=None):
    M, K = x.shape
    tm = _row_tile(M, 1024, rows_per_mod)
    tn = next((t for t in (tn_pref, 256, LANES) if n_cols % t == 0 and col_off % t == 0), n_cols)
    assert col_off % tn == 0
    joff = col_off // tn
    ni, nj = M // tm, n_cols // tn
    lead = tuple(w_lead)
    w_spec = pl.BlockSpec((None,) * len(lead) + (K, tn), lambda j, i: lead + (0, joff + j))
    in_specs = [pl.BlockSpec((tm, K), lambda j, i: (i, 0)), w_spec]
    if lane_shift:
        assert 0 < lane_shift < LANES and tn % LANES == 0
        per = tn // LANES
        in_specs.append(pl.BlockSpec((None,) * len(lead) + (K, LANES), lambda j, i: lead + (0, (joff + j + 1) * per)))
        extras = (w,) + tuple(extras)
    o_spec = pl.BlockSpec((tm, tn), lambda j, i: (i, j))
    aliases = {}
    if mode == "f32":
        out_specs, out_shape = [o_spec], [jax.ShapeDtypeStruct((M, n_cols), F32)]
    elif mode == "bf16":
        out_specs, out_shape = [o_spec], [jax.ShapeDtypeStruct((M, n_cols), BF16)]
    elif mode == "both":
        out_specs = [o_spec, o_spec]
        out_shape = [jax.ShapeDtypeStruct((M, n_cols), F32), jax.ShapeDtypeStruct((M, n_cols), BF16)]
        if stack is not None:
            buf, slab, n_slabs = stack
            out_specs[0] = pl.BlockSpec((None, tm, tn), lambda j, i: (slab, i, j))
            out_shape[0] = jax.ShapeDtypeStruct((n_slabs, M, n_cols), F32)
            if buf is not None:
                in_specs.append(pl.BlockSpec(memory_space=pl.ANY))
                extras = tuple(extras) + (buf,)
                aliases = {len(in_specs) - 1: 0}
    elif mode == "logf":
        in_specs.append(pl.BlockSpec((1, tn), lambda j, i: (0, j)))
        out_specs, out_shape = [o_spec], [jax.ShapeDtypeStruct((M, n_cols), F32)]
    elif mode == "resid":
        in_specs.append(o_spec)
        if rows_per_mod == 1:
            in_specs.append(pl.BlockSpec((tm, tn), lambda j, i: (i, j)))
        else:
            in_specs.append(pl.BlockSpec((None, 1, tn), lambda j, i: ((i * tm) // rows_per_mod, 0, j)))
        out_specs, out_shape = [o_spec], [jax.ShapeDtypeStruct((M, n_cols), F32)]
    outs = pl.pallas_call(
        functools.partial(_mm_kernel, mode=mode, lane_shift=lane_shift, aliased=bool(aliases)),
        grid=(nj, ni),
        in_specs=in_specs,
        out_specs=out_specs,
        out_shape=out_shape,
        scratch_shapes=[pltpu.VMEM((K, tn), BF16)],
        input_output_aliases=aliases,
        compiler_params=_params(("arbitrary", "arbitrary"), 56),
        name="matmul_" + mode,
    )(x, w, *extras)
    return outs if len(outs) > 1 else outs[0]


def _cumsum_kernel(lf_ref, o_ref, *, H):
    x = lf_ref[...].T[:H]
    T = x.shape[1]
    lane = lax.broadcasted_iota(I32, x.shape, 1)
    d = 1
    while d < T:
        x = x + jnp.where(lane >= d, pltpu.roll(x, d, axis=1), 0.0)
        d *= 2
    o_ref[...] = x


def _cumsum_t(logf_pad, B, T, H):
    return pl.pallas_call(
        functools.partial(_cumsum_kernel, H=H),
        grid=(B,),
        in_specs=[pl.BlockSpec((T, LANES), lambda b: (b, 0))],
        out_specs=pl.BlockSpec((None, H, T), lambda b: (b, 0, 0)),
        out_shape=jax.ShapeDtypeStruct((B, H, T), F32),
        compiler_params=_params(("arbitrary",), 32),
        name="logf_cumsum",
    )(logf_pad)


def _fox_kernel(q_ref, k_ref, v_ref, c_ref, o_ref, *, tq, nq, scale):
    Dh = q_ref.shape[1]

    def run(n_blocks):
        q = q_ref[...]
        m = jnp.full((tq, 1), -jnp.inf, F32)
        l = jnp.zeros((tq, 1), F32)
        acc = jnp.zeros((tq, Dh), F32)
        for j in range(n_blocks):
            cols = slice(j * tq, (j + 1) * tq)
            s = lax.dot_general(q, k_ref[cols, :], (((1,), (1,)), ((), ())), preferred_element_type=F32) * scale
            s = s - c_ref[:, cols]
            if j == n_blocks - 1:
                row = lax.broadcasted_iota(I32, s.shape, 0)
                col = lax.broadcasted_iota(I32, s.shape, 1)
                s = jnp.where(col <= row, s, -jnp.inf)
            m_new = jnp.maximum(m, jnp.max(s, axis=-1, keepdims=True))
            alpha = jnp.exp(m - m_new)
            p = jnp.exp(s - m_new)
            l = alpha * l + jnp.sum(p, axis=-1, keepdims=True)
            acc = alpha * acc + jnp.dot(p.astype(BF16), v_ref[cols, :], preferred_element_type=F32)
            m = m_new
        o_ref[...] = acc / l

    qi = pl.program_id(2)
    for qv in range(nq):
        pl.when(qi == qv)(functools.partial(run, qv + 1))


def _fox_prompt(q, k, v, ct, B, T, H, Dh):
    tq = _tile(T, 512)
    nq = T // tq
    return pl.pallas_call(
        functools.partial(_fox_kernel, tq=tq, nq=nq, scale=Dh ** -0.5),
        grid=(B, H, nq),
        in_specs=[
            pl.BlockSpec((tq, Dh), lambda b, h, i: (b * nq + i, h)),
            pl.BlockSpec((T, Dh), lambda b, h, i: (b, h)),
            pl.BlockSpec((T, Dh), lambda b, h, i: (b, h)),
            pl.BlockSpec((None, 1, T), lambda b, h, i: (b * H + h, 0, 0)),
        ],
        out_specs=pl.BlockSpec((tq, Dh), lambda b, h, i: (b * nq + i, h)),
        out_shape=jax.ShapeDtypeStruct((B * T, H * Dh), F32),
        compiler_params=_params(("arbitrary", "arbitrary", "arbitrary"), 32),
        name="fox_prompt",
    )(q, k, v, ct.reshape(B * H, 1, T))


def _seg_prefix(x, stride):
    W = x.shape[1]
    lane = lax.broadcasted_iota(I32, x.shape, 1)
    d = stride
    while d < W:
        x = x + jnp.where(lane >= d, pltpu.roll(x, d, axis=1), 0.0)
        d *= 2
    return x


def _seg_allreduce(x, stride, op):
    W = x.shape[1]
    d = stride
    while d < W:
        x = op(x, pltpu.roll(x, d, axis=1))
        d *= 2
    return x


def _decode_kernel(pt_ref, q_ref, kn_ref, vn_ref, lfn_ref, *rest, NP, PS, H, G, scale):
    k_refs, v_refs, f_refs = rest[:G], rest[G:2 * G], rest[2 * G:3 * G]
    o_ref = rest[3 * G]
    s_ref, lf_ref, p_ref, acc_ref, pn_ref = rest[3 * G + 1:]
    ph = pl.program_id(1)
    pg = pl.program_id(2)
    W = PS * H
    sub = lax.broadcasted_iota(I32, (H, W), 0)
    lane = lax.broadcasted_iota(I32, (H, W), 1)
    own = (lane % H) == sub

    @pl.when(ph == 0)
    def _():
        qb = q_ref[...].astype(BF16)
        for g in range(G):
            kb = k_refs[g][...].reshape(W, -1).astype(BF16)
            s = lax.dot_general(qb, kb, (((1,), (1,)), ((), ())), preferred_element_type=F32)
            row = pg * G + g
            s_ref[pl.ds(row, 1), :] = jnp.sum(jnp.where(own, s, 0.0), axis=0, keepdims=True)
            lf_ref[pl.ds(row, 1), :] = f_refs[g][...]

    @pl.when(ph == 1)
    def _():
        @pl.when(pg == 0)
        def _():
            lf = lf_ref[...]
            within = _seg_prefix(lf, H)
            page_tot = _seg_allreduce(lf, H, jnp.add)
            before = (lax.broadcasted_iota(I32, (NP, NP), 1) < lax.broadcasted_iota(I32, (NP, NP), 0)).astype(F32)
            carry = jnp.dot(before, page_tot, precision=lax.Precision.HIGHEST, preferred_element_type=F32)
            c_past = within + carry
            c_new = carry[NP - 1:NP] + page_tot[NP - 1:NP] + lfn_ref[...]
            logits = s_ref[...] * scale + (c_new - c_past)
            q_r = q_ref[...].astype(BF16).astype(F32)
            kn_r = kn_ref[...].astype(BF16).astype(F32)
            s_new_col = jnp.sum(q_r * kn_r, axis=-1, keepdims=True) * scale
            s_new = jnp.sum(jnp.where(own, s_new_col, 0.0), axis=0, keepdims=True)
            m = _seg_allreduce(jnp.max(logits, axis=0, keepdims=True), H, jnp.maximum)
            m = jnp.maximum(m, s_new)
            e = jnp.exp(logits - m)
            e_new = jnp.exp(s_new - m)
            denom = _seg_allreduce(jnp.sum(e, axis=0, keepdims=True), H, jnp.add) + e_new
            inv = 1.0 / denom
            p_ref[...] = e * inv
            pn_ref[...] = jnp.sum(jnp.where(lane == sub, e_new * inv, 0.0), axis=1, keepdims=True)
            acc_ref[...] = jnp.zeros_like(acc_ref)

        acc = acc_ref[...]
        for g in range(G):
            row = pg * G + g
            pe = jnp.where(own, jnp.broadcast_to(p_ref[pl.ds(row, 1), :], (H, W)), 0.0).astype(BF16)
            vb = v_refs[g][...].reshape(W, -1).astype(BF16)
            acc = acc + jnp.dot(pe, vb, preferred_element_type=F32)
        acc_ref[...] = acc

        @pl.when(pg == NP // G - 1)
        def _():
            o_ref[...] = acc + pn_ref[...].astype(BF16).astype(F32) * vn_ref[...].astype(BF16).astype(F32)


def _fox_decode(q, kn, vn, lf_new, cache_k, cache_v, cache_logf, page_table, l):
    Bs, H, Dh = q.shape
    NP = page_table.shape[1]
    PS = cache_k.shape[2]
    W = PS * H
    assert PS & (PS - 1) == 0, "the strided lane scans need a power-of-two page size"
    G = 4 if NP % 4 == 0 else 1
    lf_pages = cache_logf.reshape(cache_logf.shape[0], cache_logf.shape[1], 1, W)
    lfn = jnp.tile(lf_new, (1, PS)).reshape(Bs, 1, W)

    def page(b, ph, pg, pt, g, frozen_phase, frozen_at):
        return pt[b * NP + jnp.where(ph == frozen_phase, frozen_at + g, pg * G + g)]

    tok = lambda b, ph, pg, pt: (b, 0, 0)
    k_specs = [pl.BlockSpec((None, None, PS, H, Dh),
                            lambda b, ph, pg, pt, g=g: (l, page(b, ph, pg, pt, g, 1, NP - G), 0, 0, 0)) for g in range(G)]
    v_specs = [pl.BlockSpec((None, None, PS, H, Dh),
                            lambda b, ph, pg, pt, g=g: (l, page(b, ph, pg, pt, g, 0, 0), 0, 0, 0)) for g in range(G)]
    f_specs = [pl.BlockSpec((None, None, 1, W),
                            lambda b, ph, pg, pt, g=g: (l, page(b, ph, pg, pt, g, 1, NP - G), 0, 0)) for g in range(G)]
    grid_spec = pltpu.PrefetchScalarGridSpec(
        num_scalar_prefetch=1,
        grid=(Bs, 2, NP // G),
        in_specs=[pl.BlockSpec((None, H, Dh), tok), pl.BlockSpec((None, H, Dh), tok), pl.BlockSpec((None, H, Dh), tok),
                  pl.BlockSpec((None, 1, W), tok)] + k_specs + v_specs + f_specs,
        out_specs=pl.BlockSpec((None, H, Dh), tok),
        scratch_shapes=[pltpu.VMEM((NP, W), F32), pltpu.VMEM((NP, W), F32), pltpu.VMEM((NP, W), F32),
                        pltpu.VMEM((H, Dh), F32), pltpu.VMEM((H, 1), F32)],
    )
    return pl.pallas_call(
        functools.partial(_decode_kernel, NP=NP, PS=PS, H=H, G=G, scale=Dh ** -0.5),
        grid_spec=grid_spec,
        out_shape=jax.ShapeDtypeStruct((Bs, H, Dh), F32),
        compiler_params=_params(("arbitrary", "arbitrary", "arbitrary"), 48),
        name="fox_decode",
    )(page_table.reshape(-1), q, kn, vn, lfn, *([cache_k] * G), *([cache_v] * G), *([lf_pages] * G))


def _gmlp_norm_v(zv, g, b):
    gv = _gelu(zv)
    mu = jnp.mean(gv, axis=-1, keepdims=True)
    var = jnp.mean(jnp.square(gv - mu), axis=-1, keepdims=True)
    return (gv - mu) * lax.rsqrt(var + NORM_EPS) * g + b


def _gmlp_kernel(zu_ref, zv_ref, g_ref, b_ref, ws_ref, bst_ref, o_ref, *, Hg, CW):
    v = _gmlp_norm_v(zv_ref[...], g_ref[...], b_ref[...])
    C = v.shape[0]
    tril = lax.broadcasted_iota(I32, (C, C), 0) >= lax.broadcasted_iota(I32, (C, C), 1)
    for h in range(Hg):
        sl = slice(h * CW, (h + 1) * CW)
        w = jnp.where(tril, ws_ref[h], 0.0).astype(BF16)
        s = jnp.dot(w, v[:, sl].astype(BF16), preferred_element_type=F32) + bst_ref[:, h:h + 1]
        o_ref[:, sl] = _gelu(zu_ref[:, sl]) * s


def _gmlp_prompt(zg, ln_g, ln_b, ws, bs):
    M, DG2 = zg.shape
    DG = DG2 // 2
    Hg, C, _ = ws.shape
    CW = DG // Hg
    return pl.pallas_call(
        functools.partial(_gmlp_kernel, Hg=Hg, CW=CW),
        grid=(M // C,),
        in_specs=[
            pl.BlockSpec((C, DG), lambda i: (i, 0)), pl.BlockSpec((C, DG), lambda i: (i, 1)),
            pl.BlockSpec((1, DG), lambda i: (0, 0)), pl.BlockSpec((1, DG), lambda i: (0, 0)),
            pl.BlockSpec((Hg, C, C), lambda i: (0, 0, 0)), pl.BlockSpec((C, Hg), lambda i: (0, 0)),
        ],
        out_specs=pl.BlockSpec((C, DG), lambda i: (i, 0)),
        out_shape=jax.ShapeDtypeStruct((M, DG), F32),
        compiler_params=_params(("arbitrary",), 32),
        name="gmlp_prompt",
    )(zg, zg, ln_g.reshape(1, DG), ln_b.reshape(1, DG), ws, bs.T)


def _gmlp_sample_kernel(zu_ref, zv_ref, g_ref, b_ref, w0_ref, b0_ref, v_ref, o_ref):
    v = _gmlp_norm_v(zv_ref[...], g_ref[...], b_ref[...])
    v_ref[...] = v
    mix = w0_ref[...].astype(BF16).astype(F32) * v.astype(BF16).astype(F32)
    o_ref[...] = _gelu(zu_ref[...]) * (mix + b0_ref[...])


def _gmlp_sample(zg, ln_g, ln_b, ws, bs):
    R, DG2 = zg.shape
    DG = DG2 // 2
    Hg = ws.shape[0]
    CW = DG // Hg
    w0 = jnp.repeat(ws[:, 0, 0], CW).reshape(1, DG)
    b0 = jnp.repeat(bs[:, 0], CW).reshape(1, DG)
    row = pl.BlockSpec((1, DG), lambda i: (0, 0))
    return pl.pallas_call(
        _gmlp_sample_kernel,
        grid=(1,),
        in_specs=[pl.BlockSpec((R, DG), lambda i: (0, 0)), pl.BlockSpec((R, DG), lambda i: (0, 1)), row, row, row, row],
        out_specs=[pl.BlockSpec((R, DG), lambda i: (0, 0)), pl.BlockSpec((R, DG), lambda i: (0, 0))],
        out_shape=[jax.ShapeDtypeStruct((R, DG), F32), jax.ShapeDtypeStruct((R, DG), F32)],
        name="gmlp_sample",
    )(zg, zg, ln_g.reshape(1, DG), ln_b.reshape(1, DG), w0, b0)


def _merge_kernel(a_ref, b_ref, ga_ref, gb_ref, o_ref):
    DA = a_ref.shape[1]
    o_ref[:, :DA] = (_rms(a_ref[...]) * ga_ref[...]).astype(BF16)
    o_ref[:, DA:] = (_rms(b_ref[...]) * gb_ref[...]).astype(BF16)


def _merge_norm(a, b, g):
    M, DA = a.shape
    DB = b.shape[1]
    tm = _tile(M, 256)
    return pl.pallas_call(
        _merge_kernel,
        grid=(M // tm,),
        in_specs=[pl.BlockSpec((tm, DA), lambda i: (i, 0)), pl.BlockSpec((tm, DB), lambda i: (i, 0)),
                  pl.BlockSpec((1, DA), lambda i: (0, 0)), pl.BlockSpec((1, DB), lambda i: (0, 0))],
        out_specs=pl.BlockSpec((tm, DA + DB), lambda i: (i, 0)),
        out_shape=jax.ShapeDtypeStruct((M, DA + DB), BF16),
        compiler_params=_params(("arbitrary",), 32),
        name="merge_norm",
    )(a, b, g[:DA].reshape(1, DA), g[DA:].reshape(1, DB))


def _dispatch_kernel(nbu_ref, dest_ref, h_ref, o_ref, tok_ref, buf_ref, sem, *, tm, n_assign, n_rows, sentinel):
    b = pl.program_id(0)
    nbu = nbu_ref[0]

    def issue(blk, slot):
        def body(r, c):
            t = tok_ref[blk * tm + r]
            pltpu.make_async_copy(h_ref.at[pl.ds(t, 1)], buf_ref.at[slot, pl.ds(r, 1)], sem.at[slot]).start()
            return c

        lax.fori_loop(0, tm, body, 0, unroll=8)

    @pl.when(b == 0)
    def _():
        def fill(r, c):
            tok_ref[r] = sentinel
            return c

        lax.fori_loop(0, n_rows, fill, 0, unroll=8)

        def scatter(a, c):
            tok_ref[dest_ref[a]] = a // TOP_K
            return c

        lax.fori_loop(0, n_assign, scatter, 0, unroll=8)
        issue(0, 0)

    slot = b % 2

    @pl.when(b < nbu)
    def _():
        @pl.when(b + 1 < nbu)
        def _():
            issue(b + 1, 1 - slot)

        pltpu.make_async_copy(h_ref.at[pl.ds(0, tm)], buf_ref.at[slot], sem.at[slot]).wait()
        o_ref[...] = buf_ref[slot].astype(BF16)

    @pl.when(b >= nbu)
    def _():
        o_ref[...] = jnp.zeros_like(o_ref)


def _dispatch(nbu, dest, h_all, tm, nb, sentinel):
    D = h_all.shape[1]
    n_assign = dest.shape[0]
    grid_spec = pltpu.PrefetchScalarGridSpec(
        num_scalar_prefetch=2,
        grid=(nb,),
        in_specs=[pl.BlockSpec(memory_space=pl.ANY)],
        out_specs=pl.BlockSpec((tm, D), lambda b, nbu, dest: (b, 0)),
        scratch_shapes=[pltpu.SMEM((nb * tm,), I32), pltpu.VMEM((2, tm, D), F32), pltpu.SemaphoreType.DMA((2,))],
    )
    return pl.pallas_call(
        functools.partial(_dispatch_kernel, tm=tm, n_assign=n_assign, n_rows=nb * tm, sentinel=sentinel),
        grid_spec=grid_spec,
        out_shape=jax.ShapeDtypeStruct((nb * tm, D), BF16),
        compiler_params=_params(("arbitrary",), 32),
        name="moe_dispatch",
    )(nbu, dest, h_all)


def _gate_up_kernel(nbu_ref, be_ref, first_ref, x_ref, wg_ref, wu_ref, bg_ref, bu_ref, a_ref, wgb_ref, wub_ref):
    b = pl.program_id(1)

    @pl.when(b < nbu_ref[0])
    def _():
        @pl.when(first_ref[b] == 1)
        def _():
            wgb_ref[...] = wg_ref[...].astype(BF16)
            wub_ref[...] = wu_ref[...].astype(BF16)

        x = x_ref[...]
        g = jnp.dot(x, wgb_ref[...], preferred_element_type=F32) + bg_ref[...]
        u = jnp.dot(x, wub_ref[...], preferred_element_type=F32) + bu_ref[...]
        g = jnp.minimum(g, SWIGLU_LIMIT)
        u = jnp.clip(u, -SWIGLU_LIMIT, SWIGLU_LIMIT)
        a_ref[...] = (g * jax.nn.sigmoid(SWIGLU_ALPHA * g) * (u + 1.0)).astype(BF16)

    @pl.when(b >= nbu_ref[0])
    def _():
        a_ref[...] = jnp.zeros_like(a_ref)


def _gate_up(nbu, block_e, first, xs, w_gate_up, b_gate_up, l, tm):
    R, D = xs.shape
    _, E, _, DE2 = w_gate_up.shape
    DE = DE2 // 2
    tn = _tile(DE, 512)
    nj = DE // tn
    nb = R // tm

    def eff(b, nbu):
        return jnp.minimum(b, nbu[0] - 1)

    grid_spec = pltpu.PrefetchScalarGridSpec(
        num_scalar_prefetch=3,
        grid=(nj, nb),
        in_specs=[
            pl.BlockSpec((tm, D), lambda j, b, nbu, be, fi: (eff(b, nbu), 0)),
            pl.BlockSpec((None, None, D, tn), lambda j, b, nbu, be, fi: (l, be[eff(b, nbu)], 0, j)),
            pl.BlockSpec((None, None, D, tn), lambda j, b, nbu, be, fi: (l, be[eff(b, nbu)], 0, nj + j)),
            pl.BlockSpec((None, None, 1, tn), lambda j, b, nbu, be, fi: (l, be[eff(b, nbu)], 0, j)),
            pl.BlockSpec((None, None, 1, tn), lambda j, b, nbu, be, fi: (l, be[eff(b, nbu)], 0, nj + j)),
        ],
        out_specs=pl.BlockSpec((tm, tn), lambda j, b, nbu, be, fi: (b, j)),
        scratch_shapes=[pltpu.VMEM((D, tn), BF16), pltpu.VMEM((D, tn), BF16)],
    )
    bgu = b_gate_up.reshape(b_gate_up.shape[0], E, 1, DE2)
    return pl.pallas_call(
        _gate_up_kernel,
        grid_spec=grid_spec,
        out_shape=jax.ShapeDtypeStruct((R, DE), BF16),
        compiler_params=_params(("arbitrary", "arbitrary"), 56),
        name="moe_gate_up",
    )(nbu, block_e, first, xs, w_gate_up, w_gate_up, bgu, bgu)


def _down_kernel(nbu_ref, be_ref, first_ref, a_ref, w_ref, bias_ref, o_ref, wb_ref):
    b = pl.program_id(1)

    @pl.when(b < nbu_ref[0])
    def _():
        @pl.when(first_ref[b] == 1)
        def _():
            wb_ref[...] = w_ref[...].astype(BF16)

        o_ref[...] = jnp.dot(a_ref[...], wb_ref[...], preferred_element_type=F32) + bias_ref[...]

    @pl.when(b >= nbu_ref[0])
    def _():
        o_ref[...] = jnp.zeros_like(o_ref)


def _down(nbu, block_e, first, a, w_down, b_down, l, tm):
    R, DE = a.shape
    _, E, _, D = w_down.shape
    tn = _tile(D, 1024)
    nj = D // tn
    nb = R // tm

    def eff(b, nbu):
        return jnp.minimum(b, nbu[0] - 1)

    grid_spec = pltpu.PrefetchScalarGridSpec(
        num_scalar_prefetch=3,
        grid=(nj, nb),
        in_specs=[
            pl.BlockSpec((tm, DE), lambda j, b, nbu, be, fi: (eff(b, nbu), 0)),
            pl.BlockSpec((None, None, DE, tn), lambda j, b, nbu, be, fi: (l, be[eff(b, nbu)], 0, j)),
            pl.BlockSpec((None, None, 1, tn), lambda j, b, nbu, be, fi: (l, be[eff(b, nbu)], 0, j)),
        ],
        out_specs=pl.BlockSpec((tm, tn), lambda j, b, nbu, be, fi: (b, j)),
        scratch_shapes=[pltpu.VMEM((DE, tn), BF16)],
    )
    return pl.pallas_call(
        _down_kernel,
        grid_spec=grid_spec,
        out_shape=jax.ShapeDtypeStruct((R, D), F32),
        compiler_params=_params(("arbitrary", "arbitrary"), 48),
        name="moe_down",
    )(nbu, block_e, first, a, w_down, b_down.reshape(b_down.shape[0], E, 1, D))


def _combine_kernel(pos_ref, rows_ref, p_ref, x_ref, gate_ref, o_ref, buf_ref, sem, *, tc):
    i = pl.program_id(0)
    n = tc * TOP_K

    def issue(blk, slot):
        def body(a, c):
            src = pos_ref[blk * n + a]
            dst = (a % TOP_K) * tc + a // TOP_K
            pltpu.make_async_copy(rows_ref.at[pl.ds(src, 1)], buf_ref.at[slot, pl.ds(dst, 1)], sem.at[slot]).start()
            return c

        lax.fori_loop(0, n, body, 0, unroll=8)

    @pl.when(i == 0)
    def _():
        issue(0, 0)

    slot = i % 2

    @pl.when(i + 1 < pl.num_programs(0))
    def _():
        issue(i + 1, 1 - slot)

    pltpu.make_async_copy(rows_ref.at[pl.ds(0, n)], buf_ref.at[slot], sem.at[slot]).wait()
    y = p_ref[:, 0:1] * buf_ref[slot, pl.ds(0, tc)]
    for k in range(1, TOP_K):
        y = y + p_ref[:, k:k + 1] * buf_ref[slot, pl.ds(k * tc, tc)]
    o_ref[...] = x_ref[...] + gate_ref[...] * y


def _combine(pos, rows, probs, x, gate, rows_per_mod):
    M, D = x.shape
    tc = _row_tile(M, 128, rows_per_mod)
    gate_shape, gate_index = _mod_block(tc, D, rows_per_mod)
    grid_spec = pltpu.PrefetchScalarGridSpec(
        num_scalar_prefetch=1,
        grid=(M // tc,),
        in_specs=[
            pl.BlockSpec(memory_space=pl.ANY),
            pl.BlockSpec((tc, LANES), lambda i, pos: (i, 0)),
            pl.BlockSpec((tc, D), lambda i, pos: (i, 0)),
            pl.BlockSpec(gate_shape, lambda i, pos: gate_index(i)),
        ],
        out_specs=pl.BlockSpec((tc, D), lambda i, pos: (i, 0)),
        scratch_shapes=[pltpu.VMEM((2, TOP_K * tc, D), F32), pltpu.SemaphoreType.DMA((2,))],
    )
    return pl.pallas_call(
        functools.partial(_combine_kernel, tc=tc),
        grid_spec=grid_spec,
        out_shape=jax.ShapeDtypeStruct((M, D), F32),
        compiler_params=_params(("arbitrary",), 48),
        name="moe_combine",
    )(pos, rows, probs, x, gate)


def _route(top_e, rank, counts, tm):
    E = counts.shape[0]
    n_assign = top_e.shape[0] * TOP_K
    padded = (counts + tm - 1) // tm * tm
    pad_end = jnp.cumsum(padded)
    pad_start = pad_end - padded
    flat_e = top_e.reshape(n_assign)
    start = jnp.sum(jnp.where(flat_e[:, None] == jnp.arange(E, dtype=I32)[None, :], pad_start[None, :], 0), axis=1)
    dest = (start + rank.reshape(n_assign)).astype(I32)
    nb = -(-n_assign // tm) + E
    blk_row = jnp.arange(nb, dtype=I32) * tm
    block_e = jnp.minimum(jnp.sum((pad_end[None, :] <= blk_row[:, None]).astype(I32), axis=1), E - 1).astype(I32)
    first = jnp.concatenate([jnp.ones((1,), I32), (block_e[1:] != block_e[:-1]).astype(I32)])
    nbu = (pad_end[-1] // tm).astype(I32).reshape(1)
    return dest, block_e, first, nbu, nb


def kernel(x_prompt, x_sample, c_prompt, c_sample, cache_k, cache_v, cache_logf, page_table, ada_w, ada_b, norm1_g, norm2_g, w_in, forget_b, gmlp_ln_g, gmlp_ln_b, gmlp_ws, gmlp_bs, out_g, w_out, router_w, router_b, w_gate_up, b_gate_up, w_down, b_down, final_g):
    B, T, D = x_prompt.shape
    Bs, Ss, _ = x_sample.shape
    assert Ss == 1, "the sample group carries one new token per sequence"
    L = ada_w.shape[0]
    H, Dh = cache_k.shape[3], cache_k.shape[4]
    DF = H * Dh
    DG = gmlp_ln_g.shape[1]
    E = router_w.shape[2]
    M = B * T
    SP = -(-Bs // ROW_PAD) * ROW_PAD
    assert (3 * DF) % LANES == 0 and H <= LANES and E <= LANES
    moe_tm = 256

    n_c = B + Bs
    RC = -(-n_c // ROW_PAD) * ROW_PAD
    c_all = jnp.concatenate([c_prompt, c_sample, jnp.zeros((RC - n_c, D), F32)], axis=0)
    mod = _ada(c_all, ada_w, ada_b)

    xp = x_prompt.reshape(M, D)
    xs = jnp.pad(x_sample.reshape(Bs, D), ((0, SP - Bs), (0, 0)))
    page_table = page_table.astype(I32)

    k_all = v_all = None
    fp_l, ks_l, vs_l, fs_l, gs_l = [], [], [], [], []
    for l in range(L):
        mp = [m.reshape(B, 1, D) for m in jnp.split(mod[l, :B], 6, axis=-1)]
        ms = [jnp.pad(m, ((0, SP - Bs), (0, 0))) for m in jnp.split(mod[l, B:n_c], 6, axis=-1)]
        fb_pad = jnp.pad(forget_b[l], (0, LANES - H)).reshape(1, LANES)
        groups = (
            (xp, mp, T, True),
            (xs, ms, 1, False),
        )
        new_x = []
        for x, md, rpm, is_prompt in groups:
            sh1, sc1, g1 = md[0], md[1], md[2]
            h = _modnorm(x, norm1_g[l], sh1, sc1, rpm, BF16)
            if is_prompt:
                q_b = _matmul(h, w_in, (l,), 0, DF, "bf16")
                k_all, k_b = _matmul(h, w_in, (l,), DF, DF, "both", stack=(k_all, l, L))
                v_all, v_b = _matmul(h, w_in, (l,), 2 * DF, DF, "both", stack=(v_all, l, L))
            else:
                q_f = _matmul(h, w_in, (l,), 0, DF, "f32")
                k_f = _matmul(h, w_in, (l,), DF, DF, "f32")
                v_f = _matmul(h, w_in, (l,), 2 * DF, DF, "f32")
            logf_pad = _matmul(h, w_in, (l,), 3 * DF, LANES, "logf", extras=(fb_pad,))
            zg = _matmul(h, w_in, (l,), 3 * DF, 2 * DG, "f32", lane_shift=H)
            if is_prompt:
                ct = _cumsum_t(logf_pad, B, T, H)
                o_fox = _fox_prompt(q_b, k_b, v_b, ct, B, T, H, Dh)
                o_g = _gmlp_prompt(zg, gmlp_ln_g[l], gmlp_ln_b[l], gmlp_ws[l], gmlp_bs[l])
                fp_l.append(logf_pad[:, :H].reshape(B, T, H))
            else:
                lf_s = logf_pad[:Bs, :H]
                o_att = _fox_decode(q_f[:Bs].reshape(Bs, H, Dh), k_f[:Bs].reshape(Bs, H, Dh),
                                    v_f[:Bs].reshape(Bs, H, Dh), lf_s,
                                    cache_k, cache_v, cache_logf, page_table, l)
                o_fox = jnp.pad(o_att.reshape(Bs, DF), ((0, SP - Bs), (0, 0)))
                gv, o_g = _gmlp_sample(zg, gmlp_ln_g[l], gmlp_ln_b[l], gmlp_ws[l], gmlp_bs[l])
                ks_l.append(k_f[:Bs].reshape(Bs, 1, H, Dh))
                vs_l.append(v_f[:Bs].reshape(Bs, 1, H, Dh))
                fs_l.append(lf_s.reshape(Bs, 1, H))
                gs_l.append(gv[:Bs].reshape(Bs, 1, DG))
            on = _merge_norm(o_fox, o_g, out_g[l])
            new_x.append(_matmul(on, w_out, (l,), 0, D, "resid", extras=(x, g1), rows_per_mod=rpm))
        xp, xs = new_x

        rw_pad = jnp.pad(router_w[l], ((0, 0), (0, LANES - E)))
        rb_pad = jnp.pad(router_b[l], (0, LANES - E), constant_values=NEG_BIG).reshape(1, LANES)
        zero_counts = jnp.zeros((1, LANES), F32)
        h_all, e_p, p_p, r_p, cnt_p = _modnorm_router(xp, norm2_g[l], mp[3], mp[4], T, rw_pad, rb_pad,
                                                      zero_counts, M, M + SP)
        h_all, e_s, p_s, r_s, cnt = _modnorm_router(xs, norm2_g[l], ms[3], ms[4], 1, rw_pad, rb_pad,
                                                    cnt_p, Bs, M + SP, h_all=h_all)
        top_e = jnp.concatenate([e_p[:, :TOP_K], e_s[:Bs, :TOP_K]], axis=0)
        rank = jnp.concatenate([r_p[:, :TOP_K], r_s[:Bs, :TOP_K]], axis=0)
        dest, block_e, first, nbu, nb = _route(top_e, rank, cnt[0, :E].astype(I32), moe_tm)
        xs_sorted = _dispatch(nbu, dest, h_all, moe_tm, nb, M + Bs)
        act = _gate_up(nbu, block_e, first, xs_sorted, w_gate_up, b_gate_up, l, moe_tm)
        rows = _down(nbu, block_e, first, act, w_down, b_down, l, moe_tm)
        pos_p = dest[:M * TOP_K]
        pos_s = jnp.pad(dest[M * TOP_K:], (0, (SP - Bs) * TOP_K))
        p_s = jnp.where(jnp.arange(SP)[:, None] < Bs, p_s, 0.0)
        xp = _combine(pos_p, rows, p_p, xp, mp[5], T)
        xs = _combine(pos_s, rows, p_s, xs, ms[5], 1)

    y_prompt = _rmsnorm(xp, final_g).reshape(B, T, D)
    y_sample = _rmsnorm(xs, final_g)[:Bs].reshape(Bs, 1, D)
    return (y_prompt, y_sample, k_all.reshape(L, B, T, H, Dh), v_all.reshape(L, B, T, H, Dh), jnp.stack(fp_l),
            jnp.stack(ks_l), jnp.stack(vs_l), jnp.stack(fs_l), jnp.stack(gs_l))
```

```python
import functools

import jax
import jax.numpy as jnp
from jax import lax
from jax.experimental import pallas as pl
from jax.experimental.pallas import tpu as pltpu

F32 = jnp.float32
BF16 = jnp.bfloat16
I32 = jnp.int32

NORM_EPS = 1e-6
TOP_K = 4
SWIGLU_LIMIT = 7.0
SWIGLU_ALPHA = 1.702
LANES = 128
ROW_PAD = 16
NEG_BIG = -1e30
MIB = 1024 * 1024


def _tile(dim, pref):
    return pref if dim % pref == 0 else dim


def _row_tile(M, pref, rows_per_mod=None):
    if rows_per_mod in (None, 1):
        return _tile(M, pref)
    assert M % rows_per_mod == 0
    return _tile(rows_per_mod, pref)


def _params(sem, vmem_mib):
    return pltpu.CompilerParams(dimension_semantics=sem, vmem_limit_bytes=vmem_mib * MIB)


def _gelu(x):
    return 0.5 * x * (1.0 + lax.erf(x * 0.7071067811865476))


def _rms(x):
    return x * lax.rsqrt(jnp.mean(x * x, axis=-1, keepdims=True) + NORM_EPS)


def _ada_kernel(c_ref, w_ref, b_ref, o_ref):
    c = c_ref[...]
    a = (c * jax.nn.sigmoid(c)).astype(BF16)
    o_ref[...] = jnp.dot(a, w_ref[...].astype(BF16), preferred_element_type=F32) + b_ref[...]


def _ada(c_all, ada_w, ada_b):
    L, D, N = ada_w.shape
    R = c_all.shape[0]
    tn = _tile(N, 512)
    return pl.pallas_call(
        _ada_kernel,
        grid=(L, N // tn),
        in_specs=[
            pl.BlockSpec((R, D), lambda l, j: (0, 0)),
            pl.BlockSpec((None, D, tn), lambda l, j: (l, 0, j)),
            pl.BlockSpec((None, 1, tn), lambda l, j: (l, 0, j)),
        ],
        out_specs=pl.BlockSpec((None, R, tn), lambda l, j: (l, 0, j)),
        out_shape=jax.ShapeDtypeStruct((L, R, N), F32),
        compiler_params=_params(("arbitrary", "arbitrary"), 40),
        name="ada",
    )(c_all, ada_w, ada_b.reshape(L, 1, N))


def _modnorm_kernel(x_ref, g_ref, sh_ref, sc_ref, h_ref):
    h = (_rms(x_ref[...]) * g_ref[...]) * (1.0 + sc_ref[...]) + sh_ref[...]
    h_ref[...] = h.astype(h_ref.dtype)


def _router_kernel(x_ref, g_ref, sh_ref, sc_ref, rw_ref, rb_ref, c0_ref, *rest, n_valid, aliased):
    h_ref, e_ref, p_ref, r_ref, cnt_ref, carry_ref = rest[1:] if aliased else rest

    @pl.when(pl.program_id(0) == 0)
    def _():
        carry_ref[...] = c0_ref[...]

    h = (_rms(x_ref[...]) * g_ref[...]) * (1.0 + sc_ref[...]) + sh_ref[...]
    h_ref[...] = h
    logits = jnp.dot(h.astype(BF16), rw_ref[...].astype(BF16), preferred_element_type=F32) + rb_ref[...]
    tm = logits.shape[0]
    lane = lax.broadcasted_iota(I32, logits.shape, 1).astype(F32)
    cur = logits
    vals, idxs = [], []
    for _ in range(TOP_K):
        m = jnp.max(cur, axis=-1, keepdims=True)
        idx = jnp.min(jnp.where(cur == m, lane, float(LANES)), axis=-1, keepdims=True)
        vals.append(m)
        idxs.append(idx)
        cur = jnp.where(lane == idx, -jnp.inf, cur)
    exps = [jnp.exp(v - vals[0]) for v in vals]
    denom = exps[0]
    for e in exps[1:]:
        denom = denom + e

    valid = lax.broadcasted_iota(I32, logits.shape, 0) < n_valid
    memb = jnp.zeros(logits.shape, F32)
    for k in range(TOP_K):
        memb = jnp.where((lane == idxs[k]) & valid, 1.0, memb)
    earlier = lax.broadcasted_iota(I32, (tm, tm), 1) < lax.broadcasted_iota(I32, (tm, tm), 0)
    before = jnp.dot(jnp.where(earlier, 1.0, 0.0).astype(BF16), memb.astype(BF16),
                     preferred_element_type=F32) + carry_ref[...]
    total = carry_ref[...] + jnp.sum(memb, axis=0, keepdims=True)
    carry_ref[...] = total
    cnt_ref[...] = total

    e_out = jnp.zeros(logits.shape, F32)
    p_out = jnp.zeros(logits.shape, F32)
    r_out = jnp.zeros(logits.shape, F32)
    for k in range(TOP_K):
        rank = jnp.sum(jnp.where(lane == idxs[k], before, 0.0), axis=-1, keepdims=True)
        e_out = jnp.where(lane == float(k), idxs[k], e_out)
        p_out = jnp.where(lane == float(k), exps[k] / denom, p_out)
        r_out = jnp.where(lane == float(k), rank, r_out)
    e_ref[...] = e_out.astype(I32)
    p_ref[...] = p_out
    r_ref[...] = r_out.astype(I32)


def _mod_block(tm, D, rows_per_mod):
    if rows_per_mod == 1:
        return (tm, D), lambda i: (i, 0)
    return (None, 1, D), lambda i: ((i * tm) // rows_per_mod, 0, 0)


def _mod_specs(n_rows, tm, D, rows_per_mod):
    shape, index = _mod_block(tm, D, rows_per_mod)
    return pl.BlockSpec(shape, index)


def _modnorm(x, g, shift, scale, rows_per_mod, out_dtype):
    M, D = x.shape
    tm = _row_tile(M, 256, rows_per_mod)
    mod = _mod_specs(M, tm, D, rows_per_mod)
    return pl.pallas_call(
        _modnorm_kernel,
        grid=(M // tm,),
        in_specs=[pl.BlockSpec((tm, D), lambda i: (i, 0)), pl.BlockSpec((1, D), lambda i: (0, 0)), mod, mod],
        out_specs=pl.BlockSpec((tm, D), lambda i: (i, 0)),
        out_shape=jax.ShapeDtypeStruct((M, D), out_dtype),
        compiler_params=_params(("arbitrary",), 40),
        name="modnorm",
    )(x, g.reshape(1, D), shift, scale)


def _modnorm_router(x, g, shift, scale, rows_per_mod, rw_pad, rb_pad, counts0, n_valid, total_rows, h_all=None):
    M, D = x.shape
    tm = _row_tile(M, 256, rows_per_mod)
    mod = _mod_specs(M, tm, D, rows_per_mod)
    row_off = 0 if h_all is None else total_rows - M
    assert row_off % tm == 0
    boff = row_off // tm
    small = pl.BlockSpec((tm, LANES), lambda i: (i, 0))
    in_specs = [pl.BlockSpec((tm, D), lambda i: (i, 0)), pl.BlockSpec((1, D), lambda i: (0, 0)), mod, mod,
                pl.BlockSpec((D, LANES), lambda i: (0, 0)), pl.BlockSpec((1, LANES), lambda i: (0, 0)),
                pl.BlockSpec((1, LANES), lambda i: (0, 0))]
    args = [x, g.reshape(1, D), shift, scale, rw_pad, rb_pad, counts0]
    aliases = {}
    if h_all is not None:
        in_specs.append(pl.BlockSpec(memory_space=pl.ANY))
        args.append(h_all)
        aliases = {len(args) - 1: 0}
    return pl.pallas_call(
        functools.partial(_router_kernel, n_valid=n_valid, aliased=h_all is not None),
        grid=(M // tm,),
        in_specs=in_specs,
        out_specs=[pl.BlockSpec((tm, D), lambda i: (boff + i, 0)), small, small, small,
                   pl.BlockSpec((1, LANES), lambda i: (0, 0))],
        out_shape=[jax.ShapeDtypeStruct((total_rows, D), F32), jax.ShapeDtypeStruct((M, LANES), I32),
                   jax.ShapeDtypeStruct((M, LANES), F32), jax.ShapeDtypeStruct((M, LANES), I32),
                   jax.ShapeDtypeStruct((1, LANES), F32)],
        scratch_shapes=[pltpu.VMEM((1, LANES), F32)],
        input_output_aliases=aliases,
        compiler_params=_params(("arbitrary",), 48),
        name="modnorm_router",
    )(*args)


def _rmsnorm_kernel(x_ref, g_ref, o_ref):
    o_ref[...] = _rms(x_ref[...]) * g_ref[...]


def _rmsnorm(x, g):
    M, D = x.shape
    tm = _tile(M, 256)
    return pl.pallas_call(
        _rmsnorm_kernel,
        grid=(M // tm,),
        in_specs=[pl.BlockSpec((tm, D), lambda i: (i, 0)), pl.BlockSpec((1, D), lambda i: (0, 0))],
        out_specs=pl.BlockSpec((tm, D), lambda i: (i, 0)),
        out_shape=jax.ShapeDtypeStruct((M, D), F32),
        compiler_params=_params(("arbitrary",), 40),
        name="final_rmsnorm",
    )(x, g.reshape(1, D))


def _mm_kernel(x_ref, w_ref, *rest, mode, lane_shift, aliased, staged):
    if lane_shift:
        wn_ref, rest = rest[0], rest[1:]
    if aliased:
        rest = rest[1:]

    if staged:
        wb_ref = rest[-1]

        @pl.when(pl.program_id(1) == 0)
        def _():
            if not lane_shift:
                wb_ref[...] = w_ref[...].astype(BF16)
            else:
                K, tn = w_ref.shape
                kc = _tile(K, 512)
                for kk in range(K // kc):
                    rows = slice(kk * kc, (kk + 1) * kc)
                    wide = jnp.concatenate([w_ref[rows, :].astype(F32), wn_ref[rows, :].astype(F32)], axis=1)
                    wb_ref[rows, :] = pltpu.roll(wide, tn + LANES - lane_shift, axis=1)[:, :tn].astype(BF16)

        w = wb_ref[...]
    else:
        w = w_ref[...]
    acc = jnp.dot(x_ref[...], w, preferred_element_type=F32)
    if mode == "f32":
        rest[0][...] = acc
    elif mode == "bf16":
        rest[0][...] = acc.astype(BF16)
    elif mode == "both":
        rest[0][...] = acc
        rest[1][...] = acc.astype(BF16)
    elif mode == "logf":
        fb_ref, o_ref = rest[0], rest[1]
        o_ref[...] = jax.nn.log_sigmoid(acc + fb_ref[...])
    elif mode == "resid":
        xr_ref, gate_ref, o_ref = rest[0], rest[1], rest[2]
        o_ref[...] = xr_ref[...] + gate_ref[...] * acc
    else:
        raise ValueError(mode)


def _matmul(x, w, w_lead, col_off, n_cols, mode, extras=(), rows_per_mod=None, tn_pref=512, lane_shift=0, stack=None):
    M, K = x.shape
    tm = _row_tile(M, 1024, rows_per_mod)
    tn = next((t for t in (tn_pref, 256, LANES) if n_cols % t == 0 and col_off % t == 0), n_cols)
    assert col_off % tn == 0
    joff = col_off // tn
    ni, nj = M // tm, n_cols // tn
    lead = tuple(w_lead)
    w_spec = pl.BlockSpec((None,) * len(lead) + (K, tn), lambda j, i: lead + (0, joff + j))
    in_specs = [pl.BlockSpec((tm, K), lambda j, i: (i, 0)), w_spec]
    if lane_shift:
        assert 0 < lane_shift < LANES and tn % LANES == 0
        per = tn // LANES
        in_specs.append(pl.BlockSpec((None,) * len(lead) + (K, LANES), lambda j, i: lead + (0, (joff + j + 1) * per)))
        extras = (w,) + tuple(extras)
    staged = bool(lane_shift) or w.dtype != BF16
    o_spec = pl.BlockSpec((tm, tn), lambda j, i: (i, j))
    aliases = {}
    if mode == "f32":
        out_specs, out_shape = [o_spec], [jax.ShapeDtypeStruct((M, n_cols), F32)]
    elif mode == "bf16":
        out_specs, out_shape = [o_spec], [jax.ShapeDtypeStruct((M, n_cols), BF16)]
    elif mode == "both":
        out_specs = [o_spec, o_spec]
        out_shape = [jax.ShapeDtypeStruct((M, n_cols), F32), jax.ShapeDtypeStruct((M, n_cols), BF16)]
        if stack is not None:
            buf, slab, n_slabs = stack
            out_specs[0] = pl.BlockSpec((None, tm, tn), lambda j, i: (slab, i, j))
            out_shape[0] = jax.ShapeDtypeStruct((n_slabs, M, n_cols), F32)
            if buf is not None:
                in_specs.append(pl.BlockSpec(memory_space=pl.ANY))
                extras = tuple(extras) + (buf,)
                aliases = {len(in_specs) - 1: 0}
    elif mode == "logf":
        in_specs.append(pl.BlockSpec((1, tn), lambda j, i: (0, j)))
        out_specs, out_shape = [o_spec], [jax.ShapeDtypeStruct((M, n_cols), F32)]
    elif mode == "resid":
        in_specs.append(o_spec)
        if rows_per_mod == 1:
            in_specs.append(pl.BlockSpec((tm, tn), lambda j, i: (i, j)))
        else:
            in_specs.append(pl.BlockSpec((None, 1, tn), lambda j, i: ((i * tm) // rows_per_mod, 0, j)))
        out_specs, out_shape = [o_spec], [jax.ShapeDtypeStruct((M, n_cols), F32)]
    outs = pl.pallas_call(
        functools.partial(_mm_kernel, mode=mode, lane_shift=lane_shift, aliased=bool(aliases), staged=staged),
        grid=(nj, ni),
        in_specs=in_specs,
        out_specs=out_specs,
        out_shape=out_shape,
        scratch_shapes=[pltpu.VMEM((K, tn), BF16)] if staged else [],
        input_output_aliases=aliases,
        compiler_params=_params(("arbitrary", "arbitrary"), 56),
        name="matmul_" + mode,
    )(x, w, *extras)
    return outs if len(outs) > 1 else outs[0]


def _cumsum_kernel(lf_ref, o_ref, *, H):
    x = lf_ref[...].T[:H]
    T = x.shape[1]
    lane = lax.broadcasted_iota(I32, x.shape, 1)
    d = 1
    while d < T:
        x = x + jnp.where(lane >= d, pltpu.roll(x, d, axis=1), 0.0)
        d *= 2
    o_ref[...] = x


def _cumsum_t(logf_pad, B, T, H):
    return pl.pallas_call(
        functools.partial(_cumsum_kernel, H=H),
        grid=(B,),
        in_specs=[pl.BlockSpec((T, LANES), lambda b: (b, 0))],
        out_specs=pl.BlockSpec((None, H, T), lambda b: (b, 0, 0)),
        out_shape=jax.ShapeDtypeStruct((B, H, T), F32),
        compiler_params=_params(("arbitrary",), 32),
        name="logf_cumsum",
    )(logf_pad)


def _fox_kernel(q_ref, k_ref, v_ref, c_ref, o_ref, *, tq, nq, scale):
    Dh = q_ref.shape[1]

    def run(n_blocks):
        q = q_ref[...]
        m = jnp.full((tq, 1), -jnp.inf, F32)
        l = jnp.zeros((tq, 1), F32)
        acc = jnp.zeros((tq, Dh), F32)
        for j in range(n_blocks):
            cols = slice(j * tq, (j + 1) * tq)
            s = lax.dot_general(q, k_ref[cols, :], (((1,), (1,)), ((), ())), preferred_element_type=F32) * scale
            s = s - c_ref[:, cols]
            if j == n_blocks - 1:
                row = lax.broadcasted_iota(I32, s.shape, 0)
                col = lax.broadcasted_iota(I32, s.shape, 1)
                s = jnp.where(col <= row, s, -jnp.inf)
            m_new = jnp.maximum(m, jnp.max(s, axis=-1, keepdims=True))
            alpha = jnp.exp(m - m_new)
            p = jnp.exp(s - m_new)
            l = alpha * l + jnp.sum(p, axis=-1, keepdims=True)
            acc = alpha * acc + jnp.dot(p.astype(BF16), v_ref[cols, :], preferred_element_type=F32)
            m = m_new
        o_ref[...] = acc / l

    qi = pl.program_id(2)
    for qv in range(nq):
        pl.when(qi == qv)(functools.partial(run, qv + 1))


def _fox_prompt(q, k, v, ct, B, T, H, Dh):
    tq = _tile(T, 512)
    nq = T // tq
    return pl.pallas_call(
        functools.partial(_fox_kernel, tq=tq, nq=nq, scale=Dh ** -0.5),
        grid=(B, H, nq),
        in_specs=[
            pl.BlockSpec((tq, Dh), lambda b, h, i: (b * nq + i, h)),
            pl.BlockSpec((T, Dh), lambda b, h, i: (b, h)),
            pl.BlockSpec((T, Dh), lambda b, h, i: (b, h)),
            pl.BlockSpec((None, 1, T), lambda b, h, i: (b * H + h, 0, 0)),
        ],
        out_specs=pl.BlockSpec((tq, Dh), lambda b, h, i: (b * nq + i, h)),
        out_shape=jax.ShapeDtypeStruct((B * T, H * Dh), F32),
        compiler_params=_params(("arbitrary", "arbitrary", "arbitrary"), 32),
        name="fox_prompt",
    )(q, k, v, ct.reshape(B * H, 1, T))


def _seg_prefix(x, stride):
    W = x.shape[1]
    lane = lax.broadcasted_iota(I32, x.shape, 1)
    d = stride
    while d < W:
        x = x + jnp.where(lane >= d, pltpu.roll(x, d, axis=1), 0.0)
        d *= 2
    return x


def _seg_allreduce(x, stride, op):
    W = x.shape[1]
    d = stride
    while d < W:
        x = op(x, pltpu.roll(x, d, axis=1))
        d *= 2
    return x


def _decode_kernel(pt_ref, q_ref, kn_ref, vn_ref, lfn_ref, *rest, NP, PS, H, G, scale):
    k_refs, v_refs, f_refs = rest[:G], rest[G:2 * G], rest[2 * G:3 * G]
    o_ref = rest[3 * G]
    s_ref, lf_ref, p_ref, acc_ref, pn_ref = rest[3 * G + 1:]
    ph = pl.program_id(1)
    pg = pl.program_id(2)
    W = PS * H
    sub = lax.broadcasted_iota(I32, (H, W), 0)
    lane = lax.broadcasted_iota(I32, (H, W), 1)
    own = (lane % H) == sub

    @pl.when(ph == 0)
    def _():
        qb = q_ref[...].astype(BF16)
        for g in range(G):
            kb = k_refs[g][...].reshape(W, -1).astype(BF16)
            s = lax.dot_general(qb, kb, (((1,), (1,)), ((), ())), preferred_element_type=F32)
            row = pg * G + g
            s_ref[pl.ds(row, 1), :] = jnp.sum(jnp.where(own, s, 0.0), axis=0, keepdims=True)
            lf_ref[pl.ds(row, 1), :] = f_refs[g][...]

    @pl.when(ph == 1)
    def _():
        @pl.when(pg == 0)
        def _():
            lf = lf_ref[...]
            within = _seg_prefix(lf, H)
            page_tot = _seg_allreduce(lf, H, jnp.add)
            later = (lax.broadcasted_iota(I32, (NP, NP), 1) > lax.broadcasted_iota(I32, (NP, NP), 0)).astype(F32)
            after = jnp.dot(later, page_tot, precision=lax.Precision.HIGHEST, preferred_element_type=F32)
            logits = s_ref[...] * scale + (lfn_ref[...] + after + (page_tot - within))
            q_r = q_ref[...].astype(BF16).astype(F32)
            kn_r = kn_ref[...].astype(BF16).astype(F32)
            s_new_col = jnp.sum(q_r * kn_r, axis=-1, keepdims=True) * scale
            s_new = jnp.sum(jnp.where(own, s_new_col, 0.0), axis=0, keepdims=True)
            m = _seg_allreduce(jnp.max(logits, axis=0, keepdims=True), H, jnp.maximum)
            m = jnp.maximum(m, s_new)
            e = jnp.exp(logits - m)
            e_new = jnp.exp(s_new - m)
            denom = _seg_allreduce(jnp.sum(e, axis=0, keepdims=True), H, jnp.add) + e_new
            p_ref[...] = e / denom
            pn_ref[...] = jnp.sum(jnp.where(lane == sub, e_new / denom, 0.0), axis=1, keepdims=True)
            acc_ref[...] = jnp.zeros_like(acc_ref)

        acc = acc_ref[...]
        for g in range(G):
            row = pg * G + g
            pe = jnp.where(own, jnp.broadcast_to(p_ref[pl.ds(row, 1), :], (H, W)), 0.0).astype(BF16)
            vb = v_refs[g][...].reshape(W, -1).astype(BF16)
            acc = acc + jnp.dot(pe, vb, preferred_element_type=F32)
        acc_ref[...] = acc

        @pl.when(pg == NP // G - 1)
        def _():
            o_ref[...] = acc + pn_ref[...].astype(BF16).astype(F32) * vn_ref[...].astype(BF16).astype(F32)


def _fox_decode(q, kn, vn, lf_new, cache_k, cache_v, cache_logf, page_table, l):
    Bs, H, Dh = q.shape
    NP = page_table.shape[1]
    PS = cache_k.shape[2]
    W = PS * H
    assert PS & (PS - 1) == 0, "the strided lane scans need a power-of-two page size"
    G = 4 if NP % 4 == 0 else 1
    lf_pages = cache_logf.reshape(cache_logf.shape[0], cache_logf.shape[1], 1, W)
    lfn = jnp.tile(lf_new, (1, PS)).reshape(Bs, 1, W)

    def page(b, ph, pg, pt, g, frozen_phase, frozen_at):
        return pt[b * NP + jnp.where(ph == frozen_phase, frozen_at + g, pg * G + g)]

    tok = lambda b, ph, pg, pt: (b, 0, 0)
    k_specs = [pl.BlockSpec((None, None, PS, H, Dh),
                            lambda b, ph, pg, pt, g=g: (l, page(b, ph, pg, pt, g, 1, NP - G), 0, 0, 0)) for g in range(G)]
    v_specs = [pl.BlockSpec((None, None, PS, H, Dh),
                            lambda b, ph, pg, pt, g=g: (l, page(b, ph, pg, pt, g, 0, 0), 0, 0, 0)) for g in range(G)]
    f_specs = [pl.BlockSpec((None, None, 1, W),
                            lambda b, ph, pg, pt, g=g: (l, page(b, ph, pg, pt, g, 1, NP - G), 0, 0)) for g in range(G)]
    grid_spec = pltpu.PrefetchScalarGridSpec(
        num_scalar_prefetch=1,
        grid=(Bs, 2, NP // G),
        in_specs=[pl.BlockSpec((None, H, Dh), tok), pl.BlockSpec((None, H, Dh), tok), pl.BlockSpec((None, H, Dh), tok),
                  pl.BlockSpec((None, 1, W), tok)] + k_specs + v_specs + f_specs,
        out_specs=pl.BlockSpec((None, H, Dh), tok),
        scratch_shapes=[pltpu.VMEM((NP, W), F32), pltpu.VMEM((NP, W), F32), pltpu.VMEM((NP, W), F32),
                        pltpu.VMEM((H, Dh), F32), pltpu.VMEM((H, 1), F32)],
    )
    return pl.pallas_call(
        functools.partial(_decode_kernel, NP=NP, PS=PS, H=H, G=G, scale=Dh ** -0.5),
        grid_spec=grid_spec,
        out_shape=jax.ShapeDtypeStruct((Bs, H, Dh), F32),
        compiler_params=_params(("arbitrary", "arbitrary", "arbitrary"), 48),
        name="fox_decode",
    )(page_table.reshape(-1), q, kn, vn, lfn, *([cache_k] * G), *([cache_v] * G), *([lf_pages] * G))


def _gmlp_norm_v(zv, g, b):
    gv = _gelu(zv)
    mu = jnp.mean(gv, axis=-1, keepdims=True)
    var = jnp.mean(jnp.square(gv - mu), axis=-1, keepdims=True)
    return (gv - mu) * lax.rsqrt(var + NORM_EPS) * g + b


def _gmlp_kernel(zu_ref, zv_ref, g_ref, b_ref, ws_ref, bst_ref, o_ref, *, Hg, CW):
    v = _gmlp_norm_v(zv_ref[...], g_ref[...], b_ref[...])
    C = v.shape[0]
    tril = lax.broadcasted_iota(I32, (C, C), 0) >= lax.broadcasted_iota(I32, (C, C), 1)
    for h in range(Hg):
        sl = slice(h * CW, (h + 1) * CW)
        w = jnp.where(tril, ws_ref[h], 0.0).astype(BF16)
        s = jnp.dot(w, v[:, sl].astype(BF16), preferred_element_type=F32) + bst_ref[:, h:h + 1]
        o_ref[:, sl] = _gelu(zu_ref[:, sl]) * s


def _gmlp_prompt(zg, ln_g, ln_b, ws, bs):
    M, DG2 = zg.shape
    DG = DG2 // 2
    Hg, C, _ = ws.shape
    CW = DG // Hg
    return pl.pallas_call(
        functools.partial(_gmlp_kernel, Hg=Hg, CW=CW),
        grid=(M // C,),
        in_specs=[
            pl.BlockSpec((C, DG), lambda i: (i, 0)), pl.BlockSpec((C, DG), lambda i: (i, 1)),
            pl.BlockSpec((1, DG), lambda i: (0, 0)), pl.BlockSpec((1, DG), lambda i: (0, 0)),
            pl.BlockSpec((Hg, C, C), lambda i: (0, 0, 0)), pl.BlockSpec((C, Hg), lambda i: (0, 0)),
        ],
        out_specs=pl.BlockSpec((C, DG), lambda i: (i, 0)),
        out_shape=jax.ShapeDtypeStruct((M, DG), F32),
        compiler_params=_params(("arbitrary",), 32),
        name="gmlp_prompt",
    )(zg, zg, ln_g.reshape(1, DG), ln_b.reshape(1, DG), ws, bs.T)


def _gmlp_sample_kernel(zu_ref, zv_ref, g_ref, b_ref, w0_ref, b0_ref, v_ref, o_ref):
    v = _gmlp_norm_v(zv_ref[...], g_ref[...], b_ref[...])
    v_ref[...] = v
    mix = w0_ref[...].astype(BF16).astype(F32) * v.astype(BF16).astype(F32)
    o_ref[...] = _gelu(zu_ref[...]) * (mix + b0_ref[...])


def _gmlp_sample(zg, ln_g, ln_b, ws, bs):
    R, DG2 = zg.shape
    DG = DG2 // 2
    Hg = ws.shape[0]
    CW = DG // Hg
    w0 = jnp.repeat(ws[:, 0, 0], CW).reshape(1, DG)
    b0 = jnp.repeat(bs[:, 0], CW).reshape(1, DG)
    row = pl.BlockSpec((1, DG), lambda i: (0, 0))
    return pl.pallas_call(
        _gmlp_sample_kernel,
        grid=(1,),
        in_specs=[pl.BlockSpec((R, DG), lambda i: (0, 0)), pl.BlockSpec((R, DG), lambda i: (0, 1)), row, row, row, row],
        out_specs=[pl.BlockSpec((R, DG), lambda i: (0, 0)), pl.BlockSpec((R, DG), lambda i: (0, 0))],
        out_shape=[jax.ShapeDtypeStruct((R, DG), F32), jax.ShapeDtypeStruct((R, DG), F32)],
        name="gmlp_sample",
    )(zg, zg, ln_g.reshape(1, DG), ln_b.reshape(1, DG), w0, b0)


def _merge_kernel(a_ref, b_ref, ga_ref, gb_ref, o_ref):
    DA = a_ref.shape[1]
    o_ref[:, :DA] = (_rms(a_ref[...]) * ga_ref[...]).astype(BF16)
    o_ref[:, DA:] = (_rms(b_ref[...]) * gb_ref[...]).astype(BF16)


def _merge_norm(a, b, g):
    M, DA = a.shape
    DB = b.shape[1]
    tm = _tile(M, 256)
    return pl.pallas_call(
        _merge_kernel,
        grid=(M // tm,),
        in_specs=[pl.BlockSpec((tm, DA), lambda i: (i, 0)), pl.BlockSpec((tm, DB), lambda i: (i, 0)),
                  pl.BlockSpec((1, DA), lambda i: (0, 0)), pl.BlockSpec((1, DB), lambda i: (0, 0))],
        out_specs=pl.BlockSpec((tm, DA + DB), lambda i: (i, 0)),
        out_shape=jax.ShapeDtypeStruct((M, DA + DB), BF16),
        compiler_params=_params(("arbitrary",), 32),
        name="merge_norm",
    )(a, b, g[:DA].reshape(1, DA), g[DA:].reshape(1, DB))


def _dispatch_kernel(nbu_ref, dest_ref, h_ref, o_ref, tok_ref, buf_ref, sem, *, tm, n_assign, n_rows, sentinel):
    b = pl.program_id(0)
    nbu = nbu_ref[0]

    def issue(blk, slot):
        def body(r, c):
            t = tok_ref[blk * tm + r]
            pltpu.make_async_copy(h_ref.at[pl.ds(t, 1)], buf_ref.at[slot, pl.ds(r, 1)], sem.at[slot]).start()
            return c

        lax.fori_loop(0, tm, body, 0, unroll=8)

    @pl.when(b == 0)
    def _():
        group = 16
        assert group % TOP_K == 0

        def fill(i, c):
            for u in range(group):
                tok_ref[i * group + u] = sentinel
            return c

        lax.fori_loop(0, n_rows // group, fill, 0)

        def scatter(i, c):
            base = i * group
            rows = [dest_ref[base + u] for u in range(group)]
            for u in range(group):
                tok_ref[rows[u]] = i * (group // TOP_K) + u // TOP_K
            return c

        lax.fori_loop(0, n_assign // group, scatter, 0)

        def scatter_tail(a, c):
            tok_ref[dest_ref[a]] = lax.div(a, TOP_K)
            return c

        lax.fori_loop(n_assign // group * group, n_assign, scatter_tail, 0)
        issue(0, 0)

    slot = b % 2

    @pl.when(b < nbu)
    def _():
        @pl.when(b + 1 < nbu)
        def _():
            issue(b + 1, 1 - slot)

        pltpu.make_async_copy(h_ref.at[pl.ds(0, tm)], buf_ref.at[slot], sem.at[slot]).wait()
        o_ref[...] = buf_ref[slot].astype(BF16)

    @pl.when(b >= nbu)
    def _():
        o_ref[...] = jnp.zeros_like(o_ref)


def _dispatch(nbu, dest, h_all, tm, nb, sentinel):
    D = h_all.shape[1]
    n_assign = dest.shape[0]
    grid_spec = pltpu.PrefetchScalarGridSpec(
        num_scalar_prefetch=2,
        grid=(nb,),
        in_specs=[pl.BlockSpec(memory_space=pl.ANY)],
        out_specs=pl.BlockSpec((tm, D), lambda b, nbu, dest: (b, 0)),
        scratch_shapes=[pltpu.SMEM((nb * tm,), I32), pltpu.VMEM((2, tm, D), F32), pltpu.SemaphoreType.DMA((2,))],
    )
    return pl.pallas_call(
        functools.partial(_dispatch_kernel, tm=tm, n_assign=n_assign, n_rows=nb * tm, sentinel=sentinel),
        grid_spec=grid_spec,
        out_shape=jax.ShapeDtypeStruct((nb * tm, D), BF16),
        compiler_params=_params(("arbitrary",), 32),
        name="moe_dispatch",
    )(nbu, dest, h_all)


def _expert_rows_loop(n, s0, tm, x_hbm, xbuf, sin, out_hbm, obuf, sout, col0, tn, compute):
    def x_copy(i, slot):
        return pltpu.make_async_copy(x_hbm.at[pl.ds((s0 + i) * tm, tm)], xbuf.at[slot], sin.at[slot])

    def o_copy(i, slot):
        return pltpu.make_async_copy(obuf.at[slot], out_hbm.at[pl.ds((s0 + i) * tm, tm), pl.ds(col0, tn)], sout.at[slot])

    x_copy(0, 0).start()

    def body(i, c):
        slot = i % 2

        @pl.when(i + 1 < n)
        def _():
            x_copy(i + 1, 1 - slot).start()

        x_copy(i, slot).wait()

        @pl.when(i >= 2)
        def _():
            o_copy(i - 2, slot).wait()

        obuf[slot] = compute(xbuf[slot])
        o_copy(i, slot).start()
        return c

    lax.fori_loop(0, n, body, 0)

    @pl.when(n >= 2)
    def _():
        o_copy(n - 2, n % 2).wait()

    o_copy(n - 1, (n - 1) % 2).wait()


def _zero_tail(nbu, nb, tm, out_hbm, obuf, sout, col0, tn):
    obuf[0] = jnp.zeros(obuf.shape[1:], obuf.dtype)

    def body(blk, c):
        cp = pltpu.make_async_copy(obuf.at[0], out_hbm.at[pl.ds(blk * tm, tm), pl.ds(col0, tn)], sout.at[0])
        cp.start()
        cp.wait()
        return c

    lax.fori_loop(nbu, nb, body, 0)


def _gate_up_kernel(start_ref, cnt_ref, nbu_ref, x_hbm, wg_ref, wu_ref, bg_ref, bu_ref, a_hbm,
                    wgb_ref, wub_ref, xbuf, obuf, sin, sout, *, tm, tn, nb):
    j = pl.program_id(0)
    e = pl.program_id(1)
    n = cnt_ref[e]

    def compute(x):
        g = jnp.dot(x, wgb_ref[...], preferred_element_type=F32) + bg_ref[...]
        u = jnp.dot(x, wub_ref[...], preferred_element_type=F32) + bu_ref[...]
        g = jnp.minimum(g, SWIGLU_LIMIT)
        u = jnp.clip(u, -SWIGLU_LIMIT, SWIGLU_LIMIT)
        return (g * jax.nn.sigmoid(SWIGLU_ALPHA * g) * (u + 1.0)).astype(BF16)

    @pl.when(n > 0)
    def _():
        wgb_ref[...] = wg_ref[...].astype(BF16)
        wub_ref[...] = wu_ref[...].astype(BF16)
        _expert_rows_loop(n, start_ref[e], tm, x_hbm, xbuf, sin, a_hbm, obuf, sout, j * tn, tn, compute)

    @pl.when(e == pl.num_programs(1) - 1)
    def _():
        _zero_tail(nbu_ref[0], nb, tm, a_hbm, obuf, sout, j * tn, tn)


def _gate_up(blk_start, blk_cnt, nbu, xs, w_gate_up, b_gate_up, l, tm):
    R, D = xs.shape
    _, E, _, DE2 = w_gate_up.shape
    DE = DE2 // 2
    tn = _tile(DE, 512)
    nj = DE // tn
    grid_spec = pltpu.PrefetchScalarGridSpec(
        num_scalar_prefetch=3,
        grid=(nj, E),
        in_specs=[
            pl.BlockSpec(memory_space=pl.ANY),
            pl.BlockSpec((None, None, D, tn), lambda j, e, st, ct, nu: (l, e, 0, j)),
            pl.BlockSpec((None, None, D, tn), lambda j, e, st, ct, nu: (l, e, 0, nj + j)),
            pl.BlockSpec((None, None, 1, tn), lambda j, e, st, ct, nu: (l, e, 0, j)),
            pl.BlockSpec((None, None, 1, tn), lambda j, e, st, ct, nu: (l, e, 0, nj + j)),
        ],
        out_specs=pl.BlockSpec(memory_space=pl.ANY),
        scratch_shapes=[pltpu.VMEM((D, tn), BF16), pltpu.VMEM((D, tn), BF16), pltpu.VMEM((2, tm, D), BF16),
                        pltpu.VMEM((2, tm, tn), BF16), pltpu.SemaphoreType.DMA((2,)), pltpu.SemaphoreType.DMA((2,))],
    )
    bgu = b_gate_up.reshape(b_gate_up.shape[0], E, 1, DE2)
    return pl.pallas_call(
        functools.partial(_gate_up_kernel, tm=tm, tn=tn, nb=R // tm),
        grid_spec=grid_spec,
        out_shape=jax.ShapeDtypeStruct((R, DE), BF16),
        compiler_params=_params(("arbitrary", "arbitrary"), 56),
        name="moe_gate_up",
    )(blk_start, blk_cnt, nbu, xs, w_gate_up, w_gate_up, bgu, bgu)


def _down_kernel(start_ref, cnt_ref, nbu_ref, a_hbm, w_ref, bias_ref, o_hbm, wb_ref, xbuf, obuf, sin, sout, *, tm, tn, nb):
    j = pl.program_id(0)
    e = pl.program_id(1)
    n = cnt_ref[e]

    def compute(a):
        return jnp.dot(a, wb_ref[...], preferred_element_type=F32) + bias_ref[...]

    @pl.when(n > 0)
    def _():
        wb_ref[...] = w_ref[...].astype(BF16)
        _expert_rows_loop(n, start_ref[e], tm, a_hbm, xbuf, sin, o_hbm, obuf, sout, j * tn, tn, compute)

    @pl.when(e == pl.num_programs(1) - 1)
    def _():
        _zero_tail(nbu_ref[0], nb, tm, o_hbm, obuf, sout, j * tn, tn)


def _down(blk_start, blk_cnt, nbu, a, w_down, b_down, l, tm):
    R, DE = a.shape
    _, E, _, D = w_down.shape
    tn = _tile(D, 2048)
    nj = D // tn
    grid_spec = pltpu.PrefetchScalarGridSpec(
        num_scalar_prefetch=3,
        grid=(nj, E),
        in_specs=[
            pl.BlockSpec(memory_space=pl.ANY),
            pl.BlockSpec((None, None, DE, tn), lambda j, e, st, ct, nu: (l, e, 0, j)),
            pl.BlockSpec((None, None, 1, tn), lambda j, e, st, ct, nu: (l, e, 0, j)),
        ],
        out_specs=pl.BlockSpec(memory_space=pl.ANY),
        scratch_shapes=[pltpu.VMEM((DE, tn), BF16), pltpu.VMEM((2, tm, DE), BF16), pltpu.VMEM((2, tm, tn), F32),
                        pltpu.SemaphoreType.DMA((2,)), pltpu.SemaphoreType.DMA((2,))],
    )
    return pl.pallas_call(
        functools.partial(_down_kernel, tm=tm, tn=tn, nb=R // tm),
        grid_spec=grid_spec,
        out_shape=jax.ShapeDtypeStruct((R, D), F32),
        compiler_params=_params(("arbitrary", "arbitrary"), 56),
        name="moe_down",
    )(blk_start, blk_cnt, nbu, a, w_down, b_down.reshape(b_down.shape[0], E, 1, D))


def _combine_kernel(pos_ref, rows_ref, p_ref, x_ref, gate_ref, o_ref, buf_ref, sem, *, tc):
    i = pl.program_id(0)
    n = tc * TOP_K

    def issue(blk, slot):
        def body(r, c):
            for k in range(TOP_K):
                src = pos_ref[blk * n + r * TOP_K + k]
                pltpu.make_async_copy(rows_ref.at[pl.ds(src, 1)], buf_ref.at[slot, pl.ds(k * tc + r, 1)],
                                      sem.at[slot]).start()
            return c

        lax.fori_loop(0, tc, body, 0, unroll=4)

    @pl.when(i == 0)
    def _():
        issue(0, 0)

    slot = i % 2

    @pl.when(i + 1 < pl.num_programs(0))
    def _():
        issue(i + 1, 1 - slot)

    pltpu.make_async_copy(rows_ref.at[pl.ds(0, n)], buf_ref.at[slot], sem.at[slot]).wait()
    y = p_ref[:, 0:1] * buf_ref[slot, pl.ds(0, tc)]
    for k in range(1, TOP_K):
        y = y + p_ref[:, k:k + 1] * buf_ref[slot, pl.ds(k * tc, tc)]
    o_ref[...] = x_ref[...] + gate_ref[...] * y


def _combine(pos, rows, probs, x, gate, rows_per_mod):
    M, D = x.shape
    tc = _row_tile(M, 128, rows_per_mod)
    gate_shape, gate_index = _mod_block(tc, D, rows_per_mod)
    grid_spec = pltpu.PrefetchScalarGridSpec(
        num_scalar_prefetch=1,
        grid=(M // tc,),
        in_specs=[
            pl.BlockSpec(memory_space=pl.ANY),
            pl.BlockSpec((tc, LANES), lambda i, pos: (i, 0)),
            pl.BlockSpec((tc, D), lambda i, pos: (i, 0)),
            pl.BlockSpec(gate_shape, lambda i, pos: gate_index(i)),
        ],
        out_specs=pl.BlockSpec((tc, D), lambda i, pos: (i, 0)),
        scratch_shapes=[pltpu.VMEM((2, TOP_K * tc, D), F32), pltpu.SemaphoreType.DMA((2,))],
    )
    return pl.pallas_call(
        functools.partial(_combine_kernel, tc=tc),
        grid_spec=grid_spec,
        out_shape=jax.ShapeDtypeStruct((M, D), F32),
        compiler_params=_params(("arbitrary",), 48),
        name="moe_combine",
    )(pos, rows, probs, x, gate)


def _route(top_e, rank, counts, tm):
    E = counts.shape[0]
    n_assign = top_e.shape[0] * TOP_K
    padded = (counts + tm - 1) // tm * tm
    pad_end = jnp.cumsum(padded)
    pad_start = pad_end - padded
    flat_e = top_e.reshape(n_assign)
    start = jnp.sum(jnp.where(flat_e[:, None] == jnp.arange(E, dtype=I32)[None, :], pad_start[None, :], 0), axis=1)
    dest = (start + rank.reshape(n_assign)).astype(I32)
    nb = -(-n_assign // tm) + E
    nbu = (pad_end[-1] // tm).astype(I32).reshape(1)
    return dest, (pad_start // tm).astype(I32), (padded // tm).astype(I32), nbu, nb


def kernel(x_prompt, x_sample, c_prompt, c_sample, cache_k, cache_v, cache_logf, page_table, ada_w, ada_b, norm1_g, norm2_g, w_in, forget_b, gmlp_ln_g, gmlp_ln_b, gmlp_ws, gmlp_bs, out_g, w_out, router_w, router_b, w_gate_up, b_gate_up, w_down, b_down, final_g):
    B, T, D = x_prompt.shape
    Bs, Ss, _ = x_sample.shape
    assert Ss == 1, "the sample group carries one new token per sequence"
    L = ada_w.shape[0]
    H, Dh = cache_k.shape[3], cache_k.shape[4]
    DF = H * Dh
    DG = gmlp_ln_g.shape[1]
    E = router_w.shape[2]
    M = B * T
    SP = -(-Bs // ROW_PAD) * ROW_PAD
    assert (3 * DF) % LANES == 0 and H <= LANES and E <= LANES
    moe_tm = 256

    n_c = B + Bs
    RC = -(-n_c // ROW_PAD) * ROW_PAD
    c_all = jnp.concatenate([c_prompt, c_sample, jnp.zeros((RC - n_c, D), F32)], axis=0)
    mod = _ada(c_all, ada_w, ada_b)

    xp = x_prompt.reshape(M, D)
    xs = jnp.pad(x_sample.reshape(Bs, D), ((0, SP - Bs), (0, 0)))
    page_table = page_table.astype(I32)

    w_in = w_in.astype(BF16)
    k_all = v_all = None
    fp_l, ks_l, vs_l, fs_l, gs_l = [], [], [], [], []
    for l in range(L):
        mp = [m.reshape(B, 1, D) for m in jnp.split(mod[l, :B], 6, axis=-1)]
        ms = [jnp.pad(m, ((0, SP - Bs), (0, 0))) for m in jnp.split(mod[l, B:n_c], 6, axis=-1)]
        fb_pad = jnp.pad(forget_b[l], (0, LANES - H)).reshape(1, LANES)
        groups = (
            (xp, mp, T, True),
            (xs, ms, 1, False),
        )
        new_x = []
        for x, md, rpm, is_prompt in groups:
            sh1, sc1, g1 = md[0], md[1], md[2]
            h = _modnorm(x, norm1_g[l], sh1, sc1, rpm, BF16)
            if is_prompt:
                q_b = _matmul(h, w_in, (l,), 0, DF, "bf16")
                k_all, k_b = _matmul(h, w_in, (l,), DF, DF, "both", stack=(k_all, l, L))
                v_all, v_b = _matmul(h, w_in, (l,), 2 * DF, DF, "both", stack=(v_all, l, L))
            else:
                qkv = _matmul(h, w_in, (l,), 0, 3 * DF, "f32")
                q_f, k_f, v_f = qkv[:, :DF], qkv[:, DF:2 * DF], qkv[:, 2 * DF:]
            logf_pad = _matmul(h, w_in, (l,), 3 * DF, LANES, "logf", extras=(fb_pad,))
            zg = _matmul(h, w_in, (l,), 3 * DF, 2 * DG, "f32", lane_shift=H)
            if is_prompt:
                ct = _cumsum_t(logf_pad, B, T, H)
                o_fox = _fox_prompt(q_b, k_b, v_b, ct, B, T, H, Dh)
                o_g = _gmlp_prompt(zg, gmlp_ln_g[l], gmlp_ln_b[l], gmlp_ws[l], gmlp_bs[l])
                fp_l.append(logf_pad[:, :H].reshape(B, T, H))
            else:
                lf_s = logf_pad[:Bs, :H]
                o_att = _fox_decode(q_f[:Bs].reshape(Bs, H, Dh), k_f[:Bs].reshape(Bs, H, Dh),
                                    v_f[:Bs].reshape(Bs, H, Dh), lf_s,
                                    cache_k, cache_v, cache_logf, page_table, l)
                o_fox = jnp.pad(o_att.reshape(Bs, DF), ((0, SP - Bs), (0, 0)))
                gv, o_g = _gmlp_sample(zg, gmlp_ln_g[l], gmlp_ln_b[l], gmlp_ws[l], gmlp_bs[l])
                ks_l.append(k_f[:Bs].reshape(Bs, 1, H, Dh))
                vs_l.append(v_f[:Bs].reshape(Bs, 1, H, Dh))
                fs_l.append(lf_s.reshape(Bs, 1, H))
                gs_l.append(gv[:Bs].reshape(Bs, 1, DG))
            on = _merge_norm(o_fox, o_g, out_g[l])
            new_x.append(_matmul(on, w_out, (l,), 0, D, "resid", extras=(x, g1), rows_per_mod=rpm))
        xp, xs = new_x

        rw_pad = jnp.pad(router_w[l], ((0, 0), (0, LANES - E)))
        rb_pad = jnp.pad(router_b[l], (0, LANES - E), constant_values=NEG_BIG).reshape(1, LANES)
        zero_counts = jnp.zeros((1, LANES), F32)
        h_all, e_p, p_p, r_p, cnt_p = _modnorm_router(xp, norm2_g[l], mp[3], mp[4], T, rw_pad, rb_pad,
                                                      zero_counts, M, M + SP)
        h_all, e_s, p_s, r_s, cnt = _modnorm_router(xs, norm2_g[l], ms[3], ms[4], 1, rw_pad, rb_pad,
                                                    cnt_p, Bs, M + SP, h_all=h_all)
        top_e = jnp.concatenate([e_p[:, :TOP_K], e_s[:Bs, :TOP_K]], axis=0)
        rank = jnp.concatenate([r_p[:, :TOP_K], r_s[:Bs, :TOP_K]], axis=0)
        dest, blk_start, blk_cnt, nbu, nb = _route(top_e, rank, cnt[0, :E].astype(I32), moe_tm)
        xs_sorted = _dispatch(nbu, dest, h_all, moe_tm, nb, M + Bs)
        act = _gate_up(blk_start, blk_cnt, nbu, xs_sorted, w_gate_up, b_gate_up, l, moe_tm)
        rows = _down(blk_start, blk_cnt, nbu, act, w_down, b_down, l, moe_tm)
        pos_p = dest[:M * TOP_K]
        pos_s = jnp.pad(dest[M * TOP_K:], (0, (SP - Bs) * TOP_K))
        p_s = jnp.where(jnp.arange(SP)[:, None] < Bs, p_s, 0.0)
        xp = _combine(pos_p, rows, p_p, xp, mp[5], T)
        xs = _combine(pos_s, rows, p_s, xs, ms[5], 1)

    y_prompt = _rmsnorm(xp, final_g).reshape(B, T, D)
    y_sample = _rmsnorm(xs, final_g)[:Bs].reshape(Bs, 1, D)
    return (y_prompt, y_sample, k_all.reshape(L, B, T, H, Dh), v_all.reshape(L, B, T, H, Dh), jnp.stack(fp_l),
            jnp.stack(ks_l), jnp.stack(vs_l), jnp.stack(fs_l), jnp.stack(gs_l))
```

```python
import functools

import jax
import jax.numpy as jnp
from jax import lax
from jax.experimental import pallas as pl
from jax.experimental.pallas import tpu as pltpu

F32 = jnp.float32
BF16 = jnp.bfloat16
I32 = jnp.int32

NORM_EPS = 1e-6
TOP_K = 4
SWIGLU_LIMIT = 7.0
SWIGLU_ALPHA = 1.702
LANES = 128
ROW_PAD = 16
NEG_BIG = -1e30
MIB = 1024 * 1024


def _tile(dim, pref):
    return pref if dim % pref == 0 else dim


def _row_tile(M, pref, rows_per_mod=None):
    if rows_per_mod in (None, 1):
        return _tile(M, pref)
    assert M % rows_per_mod == 0
    return _tile(rows_per_mod, pref)


def _params(sem, vmem_mib):
    return pltpu.CompilerParams(dimension_semantics=sem, vmem_limit_bytes=vmem_mib * MIB)


def _gelu(x):
    return 0.5 * x * (1.0 + lax.erf(x * 0.7071067811865476))


def _rms(x):
    return x * lax.rsqrt(jnp.mean(x * x, axis=-1, keepdims=True) + NORM_EPS)


def _ada_kernel(c_ref, w_ref, b_ref, o_ref):
    c = c_ref[...]
    a = (c * jax.nn.sigmoid(c)).astype(BF16)
    o_ref[...] = jnp.dot(a, w_ref[...].astype(BF16), preferred_element_type=F32) + b_ref[...]


def _ada(c_all, ada_w, ada_b):
    L, D, N = ada_w.shape
    R = c_all.shape[0]
    tn = _tile(N, 512)
    return pl.pallas_call(
        _ada_kernel,
        grid=(L, N // tn),
        in_specs=[
            pl.BlockSpec((R, D), lambda l, j: (0, 0)),
            pl.BlockSpec((None, D, tn), lambda l, j: (l, 0, j)),
            pl.BlockSpec((None, 1, tn), lambda l, j: (l, 0, j)),
        ],
        out_specs=pl.BlockSpec((None, R, tn), lambda l, j: (l, 0, j)),
        out_shape=jax.ShapeDtypeStruct((L, R, N), F32),
        compiler_params=_params(("arbitrary", "arbitrary"), 40),
        name="ada",
    )(c_all, ada_w, ada_b.reshape(L, 1, N))


def _modnorm_kernel(x_ref, g_ref, sh_ref, sc_ref, h_ref):
    h = (_rms(x_ref[...]) * g_ref[...]) * (1.0 + sc_ref[...]) + sh_ref[...]
    h_ref[...] = h.astype(h_ref.dtype)


def _router_kernel(x_ref, g_ref, sh_ref, sc_ref, rw_ref, rb_ref, c0_ref, *rest, n_valid, aliased):
    h_ref, e_ref, p_ref, r_ref, cnt_ref, carry_ref = rest[1:] if aliased else rest

    @pl.when(pl.program_id(0) == 0)
    def _():
        carry_ref[...] = c0_ref[...]

    h = (_rms(x_ref[...]) * g_ref[...]) * (1.0 + sc_ref[...]) + sh_ref[...]
    h_ref[...] = h
    logits = jnp.dot(h.astype(BF16), rw_ref[...].astype(BF16), preferred_element_type=F32) + rb_ref[...]
    tm = logits.shape[0]
    lane = lax.broadcasted_iota(I32, logits.shape, 1).astype(F32)
    cur = logits
    vals, idxs = [], []
    for _ in range(TOP_K):
        m = jnp.max(cur, axis=-1, keepdims=True)
        idx = jnp.min(jnp.where(cur == m, lane, float(LANES)), axis=-1, keepdims=True)
        vals.append(m)
        idxs.append(idx)
        cur = jnp.where(lane == idx, -jnp.inf, cur)
    exps = [jnp.exp(v - vals[0]) for v in vals]
    denom = exps[0]
    for e in exps[1:]:
        denom = denom + e

    valid = lax.broadcasted_iota(I32, logits.shape, 0) < n_valid
    memb = jnp.zeros(logits.shape, F32)
    for k in range(TOP_K):
        memb = jnp.where((lane == idxs[k]) & valid, 1.0, memb)
    earlier = lax.broadcasted_iota(I32, (tm, tm), 1) < lax.broadcasted_iota(I32, (tm, tm), 0)
    before = jnp.dot(jnp.where(earlier, 1.0, 0.0).astype(BF16), memb.astype(BF16),
                     preferred_element_type=F32) + carry_ref[...]
    total = carry_ref[...] + jnp.sum(memb, axis=0, keepdims=True)
    carry_ref[...] = total
    cnt_ref[...] = total

    e_out = jnp.zeros(logits.shape, F32)
    p_out = jnp.zeros(logits.shape, F32)
    r_out = jnp.zeros(logits.shape, F32)
    for k in range(TOP_K):
        rank = jnp.sum(jnp.where(lane == idxs[k], before, 0.0), axis=-1, keepdims=True)
        e_out = jnp.where(lane == float(k), idxs[k], e_out)
        p_out = jnp.where(lane == float(k), exps[k] / denom, p_out)
        r_out = jnp.where(lane == float(k), rank, r_out)
    e_ref[...] = e_out.astype(I32)
    p_ref[...] = p_out
    r_ref[...] = r_out.astype(I32)


def _mod_block(tm, D, rows_per_mod):
    if rows_per_mod == 1:
        return (tm, D), lambda i: (i, 0)
    return (None, 1, D), lambda i: ((i * tm) // rows_per_mod, 0, 0)


def _mod_specs(n_rows, tm, D, rows_per_mod):
    shape, index = _mod_block(tm, D, rows_per_mod)
    return pl.BlockSpec(shape, index)


def _modnorm(x, g, shift, scale, rows_per_mod, out_dtype):
    M, D = x.shape
    tm = _row_tile(M, 256, rows_per_mod)
    mod = _mod_specs(M, tm, D, rows_per_mod)
    return pl.pallas_call(
        _modnorm_kernel,
        grid=(M // tm,),
        in_specs=[pl.BlockSpec((tm, D), lambda i: (i, 0)), pl.BlockSpec((1, D), lambda i: (0, 0)), mod, mod],
        out_specs=pl.BlockSpec((tm, D), lambda i: (i, 0)),
        out_shape=jax.ShapeDtypeStruct((M, D), out_dtype),
        compiler_params=_params(("arbitrary",), 40),
        name="modnorm",
    )(x, g.reshape(1, D), shift, scale)


def _modnorm_router(x, g, shift, scale, rows_per_mod, rw_pad, rb_pad, counts0, n_valid, total_rows, h_all=None):
    M, D = x.shape
    tm = _row_tile(M, 256, rows_per_mod)
    mod = _mod_specs(M, tm, D, rows_per_mod)
    row_off = 0 if h_all is None else total_rows - M
    assert row_off % tm == 0
    boff = row_off // tm
    small = pl.BlockSpec((tm, LANES), lambda i: (i, 0))
    in_specs = [pl.BlockSpec((tm, D), lambda i: (i, 0)), pl.BlockSpec((1, D), lambda i: (0, 0)), mod, mod,
                pl.BlockSpec((D, LANES), lambda i: (0, 0)), pl.BlockSpec((1, LANES), lambda i: (0, 0)),
                pl.BlockSpec((1, LANES), lambda i: (0, 0))]
    args = [x, g.reshape(1, D), shift, scale, rw_pad, rb_pad, counts0]
    aliases = {}
    if h_all is not None:
        in_specs.append(pl.BlockSpec(memory_space=pl.ANY))
        args.append(h_all)
        aliases = {len(args) - 1: 0}
    return pl.pallas_call(
        functools.partial(_router_kernel, n_valid=n_valid, aliased=h_all is not None),
        grid=(M // tm,),
        in_specs=in_specs,
        out_specs=[pl.BlockSpec((tm, D), lambda i: (boff + i, 0)), small, small, small,
                   pl.BlockSpec((1, LANES), lambda i: (0, 0))],
        out_shape=[jax.ShapeDtypeStruct((total_rows, D), F32), jax.ShapeDtypeStruct((M, LANES), I32),
                   jax.ShapeDtypeStruct((M, LANES), F32), jax.ShapeDtypeStruct((M, LANES), I32),
                   jax.ShapeDtypeStruct((1, LANES), F32)],
        scratch_shapes=[pltpu.VMEM((1, LANES), F32)],
        input_output_aliases=aliases,
        compiler_params=_params(("arbitrary",), 48),
        name="modnorm_router",
    )(*args)


def _rmsnorm_kernel(x_ref, g_ref, o_ref):
    o_ref[...] = _rms(x_ref[...]) * g_ref[...]


def _rmsnorm(x, g):
    M, D = x.shape
    tm = _tile(M, 256)
    return pl.pallas_call(
        _rmsnorm_kernel,
        grid=(M // tm,),
        in_specs=[pl.BlockSpec((tm, D), lambda i: (i, 0)), pl.BlockSpec((1, D), lambda i: (0, 0))],
        out_specs=pl.BlockSpec((tm, D), lambda i: (i, 0)),
        out_shape=jax.ShapeDtypeStruct((M, D), F32),
        compiler_params=_params(("arbitrary",), 40),
        name="final_rmsnorm",
    )(x, g.reshape(1, D))


def _mm_kernel(x_ref, w_ref, *rest, mode, lane_shift, aliased, staged):
    if lane_shift:
        wn_ref, rest = rest[0], rest[1:]
    if aliased:
        rest = rest[1:]

    if staged:
        wb_ref = rest[-1]

        @pl.when(pl.program_id(1) == 0)
        def _():
            if not lane_shift:
                wb_ref[...] = w_ref[...].astype(BF16)
            else:
                K, tn = w_ref.shape
                kc = _tile(K, 512)
                for kk in range(K // kc):
                    rows = slice(kk * kc, (kk + 1) * kc)
                    wide = jnp.concatenate([w_ref[rows, :].astype(F32), wn_ref[rows, :].astype(F32)], axis=1)
                    wb_ref[rows, :] = pltpu.roll(wide, tn + LANES - lane_shift, axis=1)[:, :tn].astype(BF16)

        w = wb_ref[...]
    else:
        w = w_ref[...]
    acc = jnp.dot(x_ref[...], w, preferred_element_type=F32)
    if mode == "f32":
        rest[0][...] = acc
    elif mode == "bf16":
        rest[0][...] = acc.astype(BF16)
    elif mode == "both":
        rest[0][...] = acc
        rest[1][...] = acc.astype(BF16)
    elif mode == "logf":
        fb_ref, o_ref = rest[0], rest[1]
        o_ref[...] = jax.nn.log_sigmoid(acc + fb_ref[...])
    elif mode == "resid":
        xr_ref, gate_ref, o_ref = rest[0], rest[1], rest[2]
        o_ref[...] = xr_ref[...] + gate_ref[...] * acc
    else:
        raise ValueError(mode)


def _matmul(x, w, w_lead, col_off, n_cols, mode, extras=(), rows_per_mod=None, tn_pref=512, lane_shift=0, stack=None):
    M, K = x.shape
    tm = _row_tile(M, 1024, rows_per_mod)
    tn = next((t for t in (tn_pref, 256, LANES) if n_cols % t == 0 and col_off % t == 0), n_cols)
    assert col_off % tn == 0
    joff = col_off // tn
    ni, nj = M // tm, n_cols // tn
    lead = tuple(w_lead)
    w_spec = pl.BlockSpec((None,) * len(lead) + (K, tn), lambda j, i: lead + (0, joff + j))
    in_specs = [pl.BlockSpec((tm, K), lambda j, i: (i, 0)), w_spec]
    if lane_shift:
        assert 0 < lane_shift < LANES and tn % LANES == 0
        per = tn // LANES
        in_specs.append(pl.BlockSpec((None,) * len(lead) + (K, LANES), lambda j, i: lead + (0, (joff + j + 1) * per)))
        extras = (w,) + tuple(extras)
    staged = bool(lane_shift) or w.dtype != BF16
    o_spec = pl.BlockSpec((tm, tn), lambda j, i: (i, j))
    aliases = {}
    if mode == "f32":
        out_specs, out_shape = [o_spec], [jax.ShapeDtypeStruct((M, n_cols), F32)]
    elif mode == "bf16":
        out_specs, out_shape = [o_spec], [jax.ShapeDtypeStruct((M, n_cols), BF16)]
    elif mode == "both":
        out_specs = [o_spec, o_spec]
        out_shape = [jax.ShapeDtypeStruct((M, n_cols), F32), jax.ShapeDtypeStruct((M, n_cols), BF16)]
        if stack is not None:
            buf, slab, n_slabs = stack
            out_specs[0] = pl.BlockSpec((None, tm, tn), lambda j, i: (slab, i, j))
            out_shape[0] = jax.ShapeDtypeStruct((n_slabs, M, n_cols), F32)
            if buf is not None:
                in_specs.append(pl.BlockSpec(memory_space=pl.ANY))
                extras = tuple(extras) + (buf,)
                aliases = {len(in_specs) - 1: 0}
    elif mode == "logf":
        in_specs.append(pl.BlockSpec((1, tn), lambda j, i: (0, j)))
        out_specs, out_shape = [o_spec], [jax.ShapeDtypeStruct((M, n_cols), F32)]
    elif mode == "resid":
        in_specs.append(o_spec)
        if rows_per_mod == 1:
            in_specs.append(pl.BlockSpec((tm, tn), lambda j, i: (i, j)))
        else:
            in_specs.append(pl.BlockSpec((None, 1, tn), lambda j, i: ((i * tm) // rows_per_mod, 0, j)))
        out_specs, out_shape = [o_spec], [jax.ShapeDtypeStruct((M, n_cols), F32)]
    outs = pl.pallas_call(
        functools.partial(_mm_kernel, mode=mode, lane_shift=lane_shift, aliased=bool(aliases), staged=staged),
        grid=(nj, ni),
        in_specs=in_specs,
        out_specs=out_specs,
        out_shape=out_shape,
        scratch_shapes=[pltpu.VMEM((K, tn), BF16)] if staged else [],
        input_output_aliases=aliases,
        compiler_params=_params(("arbitrary", "arbitrary"), 56),
        name="matmul_" + mode,
    )(x, w, *extras)
    return outs if len(outs) > 1 else outs[0]


def _cumsum_kernel(lf_ref, o_ref, *, H):
    x = lf_ref[...].T[:H]
    T = x.shape[1]
    lane = lax.broadcasted_iota(I32, x.shape, 1)
    d = 1
    while d < T:
        x = x + jnp.where(lane >= d, pltpu.roll(x, d, axis=1), 0.0)
        d *= 2
    o_ref[...] = x


def _cumsum_t(logf_pad, B, T, H):
    return pl.pallas_call(
        functools.partial(_cumsum_kernel, H=H),
        grid=(B,),
        in_specs=[pl.BlockSpec((T, LANES), lambda b: (b, 0))],
        out_specs=pl.BlockSpec((None, H, T), lambda b: (b, 0, 0)),
        out_shape=jax.ShapeDtypeStruct((B, H, T), F32),
        compiler_params=_params(("arbitrary",), 32),
        name="logf_cumsum",
    )(logf_pad)


def _fox_kernel(q_ref, k_ref, v_ref, c_ref, o_ref, *, tq, nq, scale):
    Dh = q_ref.shape[1]

    def run(n_blocks):
        q = q_ref[...]
        m = jnp.full((tq, 1), -jnp.inf, F32)
        l = jnp.zeros((tq, 1), F32)
        acc = jnp.zeros((tq, Dh), F32)
        for j in range(n_blocks):
            cols = slice(j * tq, (j + 1) * tq)
            s = lax.dot_general(q, k_ref[cols, :], (((1,), (1,)), ((), ())), preferred_element_type=F32) * scale
            s = s - c_ref[:, cols]
            if j == n_blocks - 1:
                row = lax.broadcasted_iota(I32, s.shape, 0)
                col = lax.broadcasted_iota(I32, s.shape, 1)
                s = jnp.where(col <= row, s, -jnp.inf)
            m_new = jnp.maximum(m, jnp.max(s, axis=-1, keepdims=True))
            alpha = jnp.exp(m - m_new)
            p = jnp.exp(s - m_new)
            l = alpha * l + jnp.sum(p, axis=-1, keepdims=True)
            acc = alpha * acc + jnp.dot(p.astype(BF16), v_ref[cols, :], preferred_element_type=F32)
            m = m_new
        o_ref[...] = acc / l

    qi = pl.program_id(2)
    for qv in range(nq):
        pl.when(qi == qv)(functools.partial(run, qv + 1))


def _fox_prompt(q, k, v, ct, B, T, H, Dh):
    tq = _tile(T, 512)
    nq = T // tq
    return pl.pallas_call(
        functools.partial(_fox_kernel, tq=tq, nq=nq, scale=Dh ** -0.5),
        grid=(B, H, nq),
        in_specs=[
            pl.BlockSpec((tq, Dh), lambda b, h, i: (b * nq + i, h)),
            pl.BlockSpec((T, Dh), lambda b, h, i: (b, h)),
            pl.BlockSpec((T, Dh), lambda b, h, i: (b, h)),
            pl.BlockSpec((None, 1, T), lambda b, h, i: (b * H + h, 0, 0)),
        ],
        out_specs=pl.BlockSpec((tq, Dh), lambda b, h, i: (b * nq + i, h)),
        out_shape=jax.ShapeDtypeStruct((B * T, H * Dh), F32),
        compiler_params=_params(("arbitrary", "arbitrary", "arbitrary"), 32),
        name="fox_prompt",
    )(q, k, v, ct.reshape(B * H, 1, T))


def _seg_prefix(x, stride):
    W = x.shape[1]
    lane = lax.broadcasted_iota(I32, x.shape, 1)
    d = stride
    while d < W:
        x = x + jnp.where(lane >= d, pltpu.roll(x, d, axis=1), 0.0)
        d *= 2
    return x


def _seg_allreduce(x, stride, op):
    W = x.shape[1]
    d = stride
    while d < W:
        x = op(x, pltpu.roll(x, d, axis=1))
        d *= 2
    return x


def _decode_kernel(pt_ref, q_ref, kn_ref, vn_ref, lfn_ref, *rest, NP, PS, H, G, scale):
    k_refs, v_refs, f_refs = rest[:G], rest[G:2 * G], rest[2 * G:3 * G]
    o_ref = rest[3 * G]
    s_ref, lf_ref, p_ref, acc_ref, pn_ref = rest[3 * G + 1:]
    ph = pl.program_id(1)
    pg = pl.program_id(2)
    W = PS * H
    sub = lax.broadcasted_iota(I32, (H, W), 0)
    lane = lax.broadcasted_iota(I32, (H, W), 1)
    own = (lane % H) == sub

    @pl.when(ph == 0)
    def _():
        qb = q_ref[...].astype(BF16)
        for g in range(G):
            kb = k_refs[g][...].reshape(W, -1).astype(BF16)
            s = lax.dot_general(qb, kb, (((1,), (1,)), ((), ())), preferred_element_type=F32)
            row = pg * G + g
            s_ref[pl.ds(row, 1), :] = jnp.sum(jnp.where(own, s, 0.0), axis=0, keepdims=True)
            lf_ref[pl.ds(row, 1), :] = f_refs[g][...]

    @pl.when(ph == 1)
    def _():
        @pl.when(pg == 0)
        def _():
            lf = lf_ref[...]
            within = _seg_prefix(lf, H)
            page_tot = _seg_allreduce(lf, H, jnp.add)
            later = (lax.broadcasted_iota(I32, (NP, NP), 1) > lax.broadcasted_iota(I32, (NP, NP), 0)).astype(F32)
            after = jnp.dot(later, page_tot, precision=lax.Precision.HIGHEST, preferred_element_type=F32)
            logits = s_ref[...] * scale + (lfn_ref[...] + after + (page_tot - within))
            q_r = q_ref[...].astype(BF16).astype(F32)
            kn_r = kn_ref[...].astype(BF16).astype(F32)
            s_new_col = jnp.sum(q_r * kn_r, axis=-1, keepdims=True) * scale
            s_new = jnp.sum(jnp.where(own, s_new_col, 0.0), axis=0, keepdims=True)
            m = _seg_allreduce(jnp.max(logits, axis=0, keepdims=True), H, jnp.maximum)
            m = jnp.maximum(m, s_new)
            e = jnp.exp(logits - m)
            e_new = jnp.exp(s_new - m)
            denom = _seg_allreduce(jnp.sum(e, axis=0, keepdims=True), H, jnp.add) + e_new
            p_ref[...] = e / denom
            pn_ref[...] = jnp.sum(jnp.where(lane == sub, e_new / denom, 0.0), axis=1, keepdims=True)
            acc_ref[...] = jnp.zeros_like(acc_ref)

        acc = acc_ref[...]
        for g in range(G):
            row = pg * G + g
            pe = jnp.where(own, jnp.broadcast_to(p_ref[pl.ds(row, 1), :], (H, W)), 0.0).astype(BF16)
            vb = v_refs[g][...].reshape(W, -1).astype(BF16)
            acc = acc + jnp.dot(pe, vb, preferred_element_type=F32)
        acc_ref[...] = acc

        @pl.when(pg == NP // G - 1)
        def _():
            o_ref[...] = acc + pn_ref[...].astype(BF16).astype(F32) * vn_ref[...].astype(BF16).astype(F32)


def _fox_decode(q, kn, vn, lf_new, cache_k, cache_v, cache_logf, page_table, l):
    Bs, H, Dh = q.shape
    NP = page_table.shape[1]
    PS = cache_k.shape[2]
    W = PS * H
    assert PS & (PS - 1) == 0, "the strided lane scans need a power-of-two page size"
    G = 4 if NP % 4 == 0 else 1
    lf_pages = cache_logf.reshape(cache_logf.shape[0], cache_logf.shape[1], 1, W)
    lfn = jnp.tile(lf_new, (1, PS)).reshape(Bs, 1, W)

    def page(b, ph, pg, pt, g, frozen_phase, frozen_at):
        return pt[b * NP + jnp.where(ph == frozen_phase, frozen_at + g, pg * G + g)]

    tok = lambda b, ph, pg, pt: (b, 0, 0)
    k_specs = [pl.BlockSpec((None, None, PS, H, Dh),
                            lambda b, ph, pg, pt, g=g: (l, page(b, ph, pg, pt, g, 1, NP - G), 0, 0, 0)) for g in range(G)]
    v_specs = [pl.BlockSpec((None, None, PS, H, Dh),
                            lambda b, ph, pg, pt, g=g: (l, page(b, ph, pg, pt, g, 0, 0), 0, 0, 0)) for g in range(G)]
    f_specs = [pl.BlockSpec((None, None, 1, W),
                            lambda b, ph, pg, pt, g=g: (l, page(b, ph, pg, pt, g, 1, NP - G), 0, 0)) for g in range(G)]
    grid_spec = pltpu.PrefetchScalarGridSpec(
        num_scalar_prefetch=1,
        grid=(Bs, 2, NP // G),
        in_specs=[pl.BlockSpec((None, H, Dh), tok), pl.BlockSpec((None, H, Dh), tok), pl.BlockSpec((None, H, Dh), tok),
                  pl.BlockSpec((None, 1, W), tok)] + k_specs + v_specs + f_specs,
        out_specs=pl.BlockSpec((None, H, Dh), tok),
        scratch_shapes=[pltpu.VMEM((NP, W), F32), pltpu.VMEM((NP, W), F32), pltpu.VMEM((NP, W), F32),
                        pltpu.VMEM((H, Dh), F32), pltpu.VMEM((H, 1), F32)],
    )
    return pl.pallas_call(
        functools.partial(_decode_kernel, NP=NP, PS=PS, H=H, G=G, scale=Dh ** -0.5),
        grid_spec=grid_spec,
        out_shape=jax.ShapeDtypeStruct((Bs, H, Dh), F32),
        compiler_params=_params(("arbitrary", "arbitrary", "arbitrary"), 48),
        name="fox_decode",
    )(page_table.reshape(-1), q, kn, vn, lfn, *([cache_k] * G), *([cache_v] * G), *([lf_pages] * G))


def _gmlp_norm_v(zv, g, b):
    gv = _gelu(zv)
    mu = jnp.mean(gv, axis=-1, keepdims=True)
    var = jnp.mean(jnp.square(gv - mu), axis=-1, keepdims=True)
    return (gv - mu) * lax.rsqrt(var + NORM_EPS) * g + b


def _gmlp_kernel(zu_ref, zv_ref, g_ref, b_ref, ws_ref, bst_ref, o_ref, *, Hg, CW):
    v = _gmlp_norm_v(zv_ref[...], g_ref[...], b_ref[...])
    C = v.shape[0]
    tril = lax.broadcasted_iota(I32, (C, C), 0) >= lax.broadcasted_iota(I32, (C, C), 1)
    for h in range(Hg):
        sl = slice(h * CW, (h + 1) * CW)
        w = jnp.where(tril, ws_ref[h], 0.0).astype(BF16)
        s = jnp.dot(w, v[:, sl].astype(BF16), preferred_element_type=F32) + bst_ref[:, h:h + 1]
        o_ref[:, sl] = _gelu(zu_ref[:, sl]) * s


def _gmlp_prompt(zg, ln_g, ln_b, ws, bs):
    M, DG2 = zg.shape
    DG = DG2 // 2
    Hg, C, _ = ws.shape
    CW = DG // Hg
    return pl.pallas_call(
        functools.partial(_gmlp_kernel, Hg=Hg, CW=CW),
        grid=(M // C,),
        in_specs=[
            pl.BlockSpec((C, DG), lambda i: (i, 0)), pl.BlockSpec((C, DG), lambda i: (i, 1)),
            pl.BlockSpec((1, DG), lambda i: (0, 0)), pl.BlockSpec((1, DG), lambda i: (0, 0)),
            pl.BlockSpec((Hg, C, C), lambda i: (0, 0, 0)), pl.BlockSpec((C, Hg), lambda i: (0, 0)),
        ],
        out_specs=pl.BlockSpec((C, DG), lambda i: (i, 0)),
        out_shape=jax.ShapeDtypeStruct((M, DG), F32),
        compiler_params=_params(("arbitrary",), 32),
        name="gmlp_prompt",
    )(zg, zg, ln_g.reshape(1, DG), ln_b.reshape(1, DG), ws, bs.T)


def _gmlp_sample_kernel(zu_ref, zv_ref, g_ref, b_ref, w0_ref, b0_ref, v_ref, o_ref):
    v = _gmlp_norm_v(zv_ref[...], g_ref[...], b_ref[...])
    v_ref[...] = v
    mix = w0_ref[...].astype(BF16).astype(F32) * v.astype(BF16).astype(F32)
    o_ref[...] = _gelu(zu_ref[...]) * (mix + b0_ref[...])


def _gmlp_sample(zg, ln_g, ln_b, ws, bs):
    R, DG2 = zg.shape
    DG = DG2 // 2
    Hg = ws.shape[0]
    CW = DG // Hg
    w0 = jnp.repeat(ws[:, 0, 0], CW).reshape(1, DG)
    b0 = jnp.repeat(bs[:, 0], CW).reshape(1, DG)
    row = pl.BlockSpec((1, DG), lambda i: (0, 0))
    return pl.pallas_call(
        _gmlp_sample_kernel,
        grid=(1,),
        in_specs=[pl.BlockSpec((R, DG), lambda i: (0, 0)), pl.BlockSpec((R, DG), lambda i: (0, 1)), row, row, row, row],
        out_specs=[pl.BlockSpec((R, DG), lambda i: (0, 0)), pl.BlockSpec((R, DG), lambda i: (0, 0))],
        out_shape=[jax.ShapeDtypeStruct((R, DG), F32), jax.ShapeDtypeStruct((R, DG), F32)],
        name="gmlp_sample",
    )(zg, zg, ln_g.reshape(1, DG), ln_b.reshape(1, DG), w0, b0)


def _merge_kernel(a_ref, b_ref, ga_ref, gb_ref, o_ref):
    DA = a_ref.shape[1]
    o_ref[:, :DA] = (_rms(a_ref[...]) * ga_ref[...]).astype(BF16)
    o_ref[:, DA:] = (_rms(b_ref[...]) * gb_ref[...]).astype(BF16)


def _merge_norm(a, b, g):
    M, DA = a.shape
    DB = b.shape[1]
    tm = _tile(M, 256)
    return pl.pallas_call(
        _merge_kernel,
        grid=(M // tm,),
        in_specs=[pl.BlockSpec((tm, DA), lambda i: (i, 0)), pl.BlockSpec((tm, DB), lambda i: (i, 0)),
                  pl.BlockSpec((1, DA), lambda i: (0, 0)), pl.BlockSpec((1, DB), lambda i: (0, 0))],
        out_specs=pl.BlockSpec((tm, DA + DB), lambda i: (i, 0)),
        out_shape=jax.ShapeDtypeStruct((M, DA + DB), BF16),
        compiler_params=_params(("arbitrary",), 32),
        name="merge_norm",
    )(a, b, g[:DA].reshape(1, DA), g[DA:].reshape(1, DB))


def _dispatch_kernel(nbu_ref, dest_ref, h_ref, o_ref, tok_ref, buf_ref, sem, *, tm, n_assign, n_rows, sentinel):
    b = pl.program_id(0)
    nbu = nbu_ref[0]

    def issue(blk, slot):
        def body(r, c):
            t = tok_ref[blk * tm + r]
            pltpu.make_async_copy(h_ref.at[pl.ds(t, 1)], buf_ref.at[slot, pl.ds(r, 1)], sem.at[slot]).start()
            return c

        lax.fori_loop(0, tm, body, 0, unroll=8)

    @pl.when(b == 0)
    def _():
        group = 16
        assert group % TOP_K == 0

        def fill(i, c):
            for u in range(group):
                tok_ref[i * group + u] = sentinel
            return c

        lax.fori_loop(0, n_rows // group, fill, 0)

        def scatter(i, c):
            base = i * group
            rows = [dest_ref[base + u] for u in range(group)]
            for u in range(group):
                tok_ref[rows[u]] = i * (group // TOP_K) + u // TOP_K
            return c

        lax.fori_loop(0, n_assign // group, scatter, 0)

        def scatter_tail(a, c):
            tok_ref[dest_ref[a]] = lax.div(a, TOP_K)
            return c

        lax.fori_loop(n_assign // group * group, n_assign, scatter_tail, 0)
        issue(0, 0)

    slot = b % 2

    @pl.when(b < nbu)
    def _():
        @pl.when(b + 1 < nbu)
        def _():
            issue(b + 1, 1 - slot)

        pltpu.make_async_copy(h_ref.at[pl.ds(0, tm)], buf_ref.at[slot], sem.at[slot]).wait()
        o_ref[...] = buf_ref[slot].astype(BF16)

    @pl.when(b >= nbu)
    def _():
        o_ref[...] = jnp.zeros_like(o_ref)


def _dispatch(nbu, dest, h_all, tm, nb, sentinel):
    D = h_all.shape[1]
    n_assign = dest.shape[0]
    grid_spec = pltpu.PrefetchScalarGridSpec(
        num_scalar_prefetch=2,
        grid=(nb,),
        in_specs=[pl.BlockSpec(memory_space=pl.ANY)],
        out_specs=pl.BlockSpec((tm, D), lambda b, nbu, dest: (b, 0)),
        scratch_shapes=[pltpu.SMEM((nb * tm,), I32), pltpu.VMEM((2, tm, D), F32), pltpu.SemaphoreType.DMA((2,))],
    )
    return pl.pallas_call(
        functools.partial(_dispatch_kernel, tm=tm, n_assign=n_assign, n_rows=nb * tm, sentinel=sentinel),
        grid_spec=grid_spec,
        out_shape=jax.ShapeDtypeStruct((nb * tm, D), BF16),
        compiler_params=_params(("arbitrary",), 32),
        name="moe_dispatch",
    )(nbu, dest, h_all)


def _expert_rows_loop(n, s0, n_next, s0_next, is_first, tm, x_hbm, xbuf, sin, out_hbm, obuf, sout, col0, tn, compute):
    def x_copy(blk, slot):
        return pltpu.make_async_copy(x_hbm.at[pl.ds(blk * tm, tm)], xbuf.at[slot], sin.at[slot])

    def o_copy(i, slot):
        return pltpu.make_async_copy(obuf.at[slot], out_hbm.at[pl.ds((s0 + i) * tm, tm), pl.ds(col0, tn)], sout.at[slot])

    for first in range(2):
        @pl.when(is_first & (n > first))
        def _():
            x_copy(s0 + first, first).start()

    def body(i, c):
        @pl.when(i + 2 < n)
        def _():
            x_copy(s0 + i + 2, (i + 2) % 3).start()

        x_copy(s0 + i, i % 3).wait()
        oslot = i % 2

        @pl.when(i >= 2)
        def _():
            o_copy(i - 2, oslot).wait()

        obuf[oslot] = compute(xbuf[i % 3])
        o_copy(i, oslot).start()
        return c

    lax.fori_loop(0, n, body, 0)

    for first in range(2):
        @pl.when(n_next > first)
        def _():
            x_copy(s0_next + first, first).start()

    @pl.when(n >= 2)
    def _():
        o_copy(n - 2, n % 2).wait()

    @pl.when(n >= 1)
    def _():
        o_copy(n - 1, (n - 1) % 2).wait()


def _next_expert(cnt_ref, start_ref, j, e):
    n_e = pl.num_programs(1)
    last = (j == pl.num_programs(0) - 1) & (e == n_e - 1)
    e_next = jnp.where(e == n_e - 1, 0, e + 1)
    return jnp.where(last, 0, cnt_ref[e_next]), start_ref[e_next], (j == 0) & (e == 0)


def _zero_tail(nbu, nb, tm, out_hbm, obuf, sout, col0, tn):
    obuf[0] = jnp.zeros(obuf.shape[1:], obuf.dtype)

    def body(blk, c):
        cp = pltpu.make_async_copy(obuf.at[0], out_hbm.at[pl.ds(blk * tm, tm), pl.ds(col0, tn)], sout.at[0])
        cp.start()
        cp.wait()
        return c

    lax.fori_loop(nbu, nb, body, 0)


def _gate_up_kernel(start_ref, cnt_ref, nbu_ref, x_hbm, wg_ref, wu_ref, bg_ref, bu_ref, a_hbm,
                    wgb_ref, wub_ref, xbuf, obuf, sin, sout, *, tm, tn, nb):
    j = pl.program_id(0)
    e = pl.program_id(1)
    n = cnt_ref[e]

    def compute(x):
        g = jnp.dot(x, wgb_ref[...], preferred_element_type=F32) + bg_ref[...]
        u = jnp.dot(x, wub_ref[...], preferred_element_type=F32) + bu_ref[...]
        g = jnp.minimum(g, SWIGLU_LIMIT)
        u = jnp.clip(u, -SWIGLU_LIMIT, SWIGLU_LIMIT)
        return (g * jax.nn.sigmoid(SWIGLU_ALPHA * g) * (u + 1.0)).astype(BF16)

    @pl.when(n > 0)
    def _():
        wgb_ref[...] = wg_ref[...].astype(BF16)
        wub_ref[...] = wu_ref[...].astype(BF16)

    n_next, s_next, is_first = _next_expert(cnt_ref, start_ref, j, e)
    _expert_rows_loop(n, start_ref[e], n_next, s_next, is_first, tm, x_hbm, xbuf, sin, a_hbm, obuf, sout,
                      j * tn, tn, compute)

    @pl.when(e == pl.num_programs(1) - 1)
    def _():
        _zero_tail(nbu_ref[0], nb, tm, a_hbm, obuf, sout, j * tn, tn)


def _gate_up(blk_start, blk_cnt, nbu, xs, w_gate_up, b_gate_up, l, tm):
    R, D = xs.shape
    _, E, _, DE2 = w_gate_up.shape
    DE = DE2 // 2
    tn = _tile(DE, 512)
    nj = DE // tn
    grid_spec = pltpu.PrefetchScalarGridSpec(
        num_scalar_prefetch=3,
        grid=(nj, E),
        in_specs=[
            pl.BlockSpec(memory_space=pl.ANY),
            pl.BlockSpec((None, None, D, tn), lambda j, e, st, ct, nu: (l, e, 0, j)),
            pl.BlockSpec((None, None, D, tn), lambda j, e, st, ct, nu: (l, e, 0, nj + j)),
            pl.BlockSpec((None, None, 1, tn), lambda j, e, st, ct, nu: (l, e, 0, j)),
            pl.BlockSpec((None, None, 1, tn), lambda j, e, st, ct, nu: (l, e, 0, nj + j)),
        ],
        out_specs=pl.BlockSpec(memory_space=pl.ANY),
        scratch_shapes=[pltpu.VMEM((D, tn), BF16), pltpu.VMEM((D, tn), BF16), pltpu.VMEM((3, tm, D), BF16),
                        pltpu.VMEM((2, tm, tn), BF16), pltpu.SemaphoreType.DMA((3,)), pltpu.SemaphoreType.DMA((2,))],
    )
    bgu = b_gate_up.reshape(b_gate_up.shape[0], E, 1, DE2)
    return pl.pallas_call(
        functools.partial(_gate_up_kernel, tm=tm, tn=tn, nb=R // tm),
        grid_spec=grid_spec,
        out_shape=jax.ShapeDtypeStruct((R, DE), BF16),
        compiler_params=_params(("arbitrary", "arbitrary"), 56),
        name="moe_gate_up",
    )(blk_start, blk_cnt, nbu, xs, w_gate_up, w_gate_up, bgu, bgu)


def _down_kernel(start_ref, cnt_ref, nbu_ref, a_hbm, w_ref, bias_ref, o_hbm, wb_ref, xbuf, obuf, sin, sout, *, tm, tn, nb):
    j = pl.program_id(0)
    e = pl.program_id(1)
    n = cnt_ref[e]

    def compute(a):
        return jnp.dot(a, wb_ref[...], preferred_element_type=F32) + bias_ref[...]

    @pl.when(n > 0)
    def _():
        wb_ref[...] = w_ref[...].astype(BF16)

    n_next, s_next, is_first = _next_expert(cnt_ref, start_ref, j, e)
    _expert_rows_loop(n, start_ref[e], n_next, s_next, is_first, tm, a_hbm, xbuf, sin, o_hbm, obuf, sout,
                      j * tn, tn, compute)

    @pl.when(e == pl.num_programs(1) - 1)
    def _():
        _zero_tail(nbu_ref[0], nb, tm, o_hbm, obuf, sout, j * tn, tn)


def _down(blk_start, blk_cnt, nbu, a, w_down, b_down, l, tm):
    R, DE = a.shape
    _, E, _, D = w_down.shape
    tn = _tile(D, 2048)
    nj = D // tn
    grid_spec = pltpu.PrefetchScalarGridSpec(
        num_scalar_prefetch=3,
        grid=(nj, E),
        in_specs=[
            pl.BlockSpec(memory_space=pl.ANY),
            pl.BlockSpec((None, None, DE, tn), lambda j, e, st, ct, nu: (l, e, 0, j)),
            pl.BlockSpec((None, None, 1, tn), lambda j, e, st, ct, nu: (l, e, 0, j)),
        ],
        out_specs=pl.BlockSpec(memory_space=pl.ANY),
        scratch_shapes=[pltpu.VMEM((DE, tn), BF16), pltpu.VMEM((3, tm, DE), BF16), pltpu.VMEM((2, tm, tn), F32),
                        pltpu.SemaphoreType.DMA((3,)), pltpu.SemaphoreType.DMA((2,))],
    )
    return pl.pallas_call(
        functools.partial(_down_kernel, tm=tm, tn=tn, nb=R // tm),
        grid_spec=grid_spec,
        out_shape=jax.ShapeDtypeStruct((R, D), F32),
        compiler_params=_params(("arbitrary", "arbitrary"), 56),
        name="moe_down",
    )(blk_start, blk_cnt, nbu, a, w_down, b_down.reshape(b_down.shape[0], E, 1, D))


def _combine_kernel(pos_ref, rows_ref, p_ref, x_ref, gate_ref, o_ref, buf_ref, sem, *, tc):
    i = pl.program_id(0)
    n = tc * TOP_K

    def issue(blk, slot):
        def body(r, c):
            for k in range(TOP_K):
                src = pos_ref[blk * n + r * TOP_K + k]
                pltpu.make_async_copy(rows_ref.at[pl.ds(src, 1)], buf_ref.at[slot, pl.ds(k * tc + r, 1)],
                                      sem.at[slot]).start()
            return c

        lax.fori_loop(0, tc, body, 0, unroll=4)

    @pl.when(i == 0)
    def _():
        issue(0, 0)

    slot = i % 2

    @pl.when(i + 1 < pl.num_programs(0))
    def _():
        issue(i + 1, 1 - slot)

    pltpu.make_async_copy(rows_ref.at[pl.ds(0, n)], buf_ref.at[slot], sem.at[slot]).wait()
    y = p_ref[:, 0:1] * buf_ref[slot, pl.ds(0, tc)]
    for k in range(1, TOP_K):
        y = y + p_ref[:, k:k + 1] * buf_ref[slot, pl.ds(k * tc, tc)]
    o_ref[...] = x_ref[...] + gate_ref[...] * y


def _combine(pos, rows, probs, x, gate, rows_per_mod):
    M, D = x.shape
    tc = _row_tile(M, 128, rows_per_mod)
    gate_shape, gate_index = _mod_block(tc, D, rows_per_mod)
    grid_spec = pltpu.PrefetchScalarGridSpec(
        num_scalar_prefetch=1,
        grid=(M // tc,),
        in_specs=[
            pl.BlockSpec(memory_space=pl.ANY),
            pl.BlockSpec((tc, LANES), lambda i, pos: (i, 0)),
            pl.BlockSpec((tc, D), lambda i, pos: (i, 0)),
            pl.BlockSpec(gate_shape, lambda i, pos: gate_index(i)),
        ],
        out_specs=pl.BlockSpec((tc, D), lambda i, pos: (i, 0)),
        scratch_shapes=[pltpu.VMEM((2, TOP_K * tc, D), F32), pltpu.SemaphoreType.DMA((2,))],
    )
    return pl.pallas_call(
        functools.partial(_combine_kernel, tc=tc),
        grid_spec=grid_spec,
        out_shape=jax.ShapeDtypeStruct((M, D), F32),
        compiler_params=_params(("arbitrary",), 48),
        name="moe_combine",
    )(pos, rows, probs, x, gate)


def _route(top_e, rank, counts, tm):
    E = counts.shape[0]
    n_assign = top_e.shape[0] * TOP_K
    padded = (counts + tm - 1) // tm * tm
    pad_end = jnp.cumsum(padded)
    pad_start = pad_end - padded
    flat_e = top_e.reshape(n_assign)
    start = jnp.sum(jnp.where(flat_e[:, None] == jnp.arange(E, dtype=I32)[None, :], pad_start[None, :], 0), axis=1)
    dest = (start + rank.reshape(n_assign)).astype(I32)
    nb = -(-n_assign // tm) + E
    nbu = (pad_end[-1] // tm).astype(I32).reshape(1)
    return dest, (pad_start // tm).astype(I32), (padded // tm).astype(I32), nbu, nb


def kernel(x_prompt, x_sample, c_prompt, c_sample, cache_k, cache_v, cache_logf, page_table, ada_w, ada_b, norm1_g, norm2_g, w_in, forget_b, gmlp_ln_g, gmlp_ln_b, gmlp_ws, gmlp_bs, out_g, w_out, router_w, router_b, w_gate_up, b_gate_up, w_down, b_down, final_g):
    B, T, D = x_prompt.shape
    Bs, Ss, _ = x_sample.shape
    assert Ss == 1, "the sample group carries one new token per sequence"
    L = ada_w.shape[0]
    H, Dh = cache_k.shape[3], cache_k.shape[4]
    DF = H * Dh
    DG = gmlp_ln_g.shape[1]
    E = router_w.shape[2]
    M = B * T
    SP = -(-Bs // ROW_PAD) * ROW_PAD
    assert (3 * DF) % LANES == 0 and H <= LANES and E <= LANES
    moe_tm = 256

    n_c = B + Bs
    RC = -(-n_c // ROW_PAD) * ROW_PAD
    c_all = jnp.concatenate([c_prompt, c_sample, jnp.zeros((RC - n_c, D), F32)], axis=0)
    mod = _ada(c_all, ada_w, ada_b)

    xp = x_prompt.reshape(M, D)
    xs = jnp.pad(x_sample.reshape(Bs, D), ((0, SP - Bs), (0, 0)))
    page_table = page_table.astype(I32)

    w_in = w_in.astype(BF16)
    k_all = v_all = None
    fp_l, ks_l, vs_l, fs_l, gs_l = [], [], [], [], []
    for l in range(L):
        mp = [m.reshape(B, 1, D) for m in jnp.split(mod[l, :B], 6, axis=-1)]
        ms = [jnp.pad(m, ((0, SP - Bs), (0, 0))) for m in jnp.split(mod[l, B:n_c], 6, axis=-1)]
        fb_pad = jnp.pad(forget_b[l], (0, LANES - H)).reshape(1, LANES)
        groups = (
            (xp, mp, T, True),
            (xs, ms, 1, False),
        )
        new_x = []
        for x, md, rpm, is_prompt in groups:
            sh1, sc1, g1 = md[0], md[1], md[2]
            h = _modnorm(x, norm1_g[l], sh1, sc1, rpm, BF16)
            if is_prompt:
                q_b = _matmul(h, w_in, (l,), 0, DF, "bf16")
                k_all, k_b = _matmul(h, w_in, (l,), DF, DF, "both", stack=(k_all, l, L))
                v_all, v_b = _matmul(h, w_in, (l,), 2 * DF, DF, "both", stack=(v_all, l, L))
            else:
                qkv = _matmul(h, w_in, (l,), 0, 3 * DF, "f32")
                q_f, k_f, v_f = qkv[:, :DF], qkv[:, DF:2 * DF], qkv[:, 2 * DF:]
            logf_pad = _matmul(h, w_in, (l,), 3 * DF, LANES, "logf", extras=(fb_pad,))
            zg = _matmul(h, w_in, (l,), 3 * DF, 2 * DG, "f32", lane_shift=H)
            if is_prompt:
                ct = _cumsum_t(logf_pad, B, T, H)
                o_fox = _fox_prompt(q_b, k_b, v_b, ct, B, T, H, Dh)
                o_g = _gmlp_prompt(zg, gmlp_ln_g[l], gmlp_ln_b[l], gmlp_ws[l], gmlp_bs[l])
                fp_l.append(logf_pad[:, :H].reshape(B, T, H))
            else:
                lf_s = logf_pad[:Bs, :H]
                o_att = _fox_decode(q_f[:Bs].reshape(Bs, H, Dh), k_f[:Bs].reshape(Bs, H, Dh),
                                    v_f[:Bs].reshape(Bs, H, Dh), lf_s,
                                    cache_k, cache_v, cache_logf, page_table, l)
                o_fox = jnp.pad(o_att.reshape(Bs, DF), ((0, SP - Bs), (0, 0)))
                gv, o_g = _gmlp_sample(zg, gmlp_ln_g[l], gmlp_ln_b[l], gmlp_ws[l], gmlp_bs[l])
                ks_l.append(k_f[:Bs].reshape(Bs, 1, H, Dh))
                vs_l.append(v_f[:Bs].reshape(Bs, 1, H, Dh))
                fs_l.append(lf_s.reshape(Bs, 1, H))
                gs_l.append(gv[:Bs].reshape(Bs, 1, DG))
            on = _merge_norm(o_fox, o_g, out_g[l])
            new_x.append(_matmul(on, w_out, (l,), 0, D, "resid", extras=(x, g1), rows_per_mod=rpm))
        xp, xs = new_x

        rw_pad = jnp.pad(router_w[l], ((0, 0), (0, LANES - E)))
        rb_pad = jnp.pad(router_b[l], (0, LANES - E), constant_values=NEG_BIG).reshape(1, LANES)
        zero_counts = jnp.zeros((1, LANES), F32)
        h_all, e_p, p_p, r_p, cnt_p = _modnorm_router(xp, norm2_g[l], mp[3], mp[4], T, rw_pad, rb_pad,
                                                      zero_counts, M, M + SP)
        h_all, e_s, p_s, r_s, cnt = _modnorm_router(xs, norm2_g[l], ms[3], ms[4], 1, rw_pad, rb_pad,
                                                    cnt_p, Bs, M + SP, h_all=h_all)
        top_e = jnp.concatenate([e_p[:, :TOP_K], e_s[:Bs, :TOP_K]], axis=0)
        rank = jnp.concatenate([r_p[:, :TOP_K], r_s[:Bs, :TOP_K]], axis=0)
        dest, blk_start, blk_cnt, nbu, nb = _route(top_e, rank, cnt[0, :E].astype(I32), moe_tm)
        xs_sorted = _dispatch(nbu, dest, h_all, moe_tm, nb, M + Bs)
        act = _gate_up(blk_start, blk_cnt, nbu, xs_sorted, w_gate_up, b_gate_up, l, moe_tm)
        rows = _down(blk_start, blk_cnt, nbu, act, w_down, b_down, l, moe_tm)
        pos_p = dest[:M * TOP_K]
        pos_s = jnp.pad(dest[M * TOP_K:], (0, (SP - Bs) * TOP_K))
        p_s = jnp.where(jnp.arange(SP)[:, None] < Bs, p_s, 0.0)
        xp = _combine(pos_p, rows, p_p, xp, mp[5], T)
        xs = _combine(pos_s, rows, p_s, xs, ms[5], 1)

    y_prompt = _rmsnorm(xp, final_g).reshape(B, T, D)
    y_sample = _rmsnorm(xs, final_g)[:Bs].reshape(Bs, 1, D)
    return (y_prompt, y_sample, k_all.reshape(L, B, T, H, Dh), v_all.reshape(L, B, T, H, Dh), jnp.stack(fp_l),
            jnp.stack(ks_l), jnp.stack(vs_l), jnp.stack(fs_l), jnp.stack(gs_l))
```

```python
import functools

import jax
import jax.numpy as jnp
from jax import lax
from jax.experimental import pallas as pl
from jax.experimental.pallas import tpu as pltpu

F32 = jnp.float32
BF16 = jnp.bfloat16
I32 = jnp.int32

NORM_EPS = 1e-6
TOP_K = 4
SWIGLU_LIMIT = 7.0
SWIGLU_ALPHA = 1.702
LANES = 128
ROW_PAD = 16
NEG_BIG = -1e30
MIB = 1024 * 1024


def _tile(dim, pref):
    return pref if dim % pref == 0 else dim


def _row_tile(M, pref, rows_per_mod=None):
    if rows_per_mod in (None, 1):
        return _tile(M, pref)
    assert M % rows_per_mod == 0
    return _tile(rows_per_mod, pref)


def _params(sem, vmem_mib):
    return pltpu.CompilerParams(dimension_semantics=sem, vmem_limit_bytes=vmem_mib * MIB)


def _gelu(x):
    return 0.5 * x * (1.0 + lax.erf(x * 0.7071067811865476))


def _rms(x):
    return x * lax.rsqrt(jnp.mean(x * x, axis=-1, keepdims=True) + NORM_EPS)


def _ada_kernel(c_ref, w_ref, b_ref, o_ref):
    c = c_ref[...]
    a = (c * jax.nn.sigmoid(c)).astype(BF16)
    o_ref[...] = jnp.dot(a, w_ref[...].astype(BF16), preferred_element_type=F32) + b_ref[...]


def _ada(c_all, ada_w, ada_b):
    L, D, N = ada_w.shape
    R = c_all.shape[0]
    tn = _tile(N, 512)
    return pl.pallas_call(
        _ada_kernel,
        grid=(L, N // tn),
        in_specs=[
            pl.BlockSpec((R, D), lambda l, j: (0, 0)),
            pl.BlockSpec((None, D, tn), lambda l, j: (l, 0, j)),
            pl.BlockSpec((None, 1, tn), lambda l, j: (l, 0, j)),
        ],
        out_specs=pl.BlockSpec((None, R, tn), lambda l, j: (l, 0, j)),
        out_shape=jax.ShapeDtypeStruct((L, R, N), F32),
        compiler_params=_params(("arbitrary", "arbitrary"), 40),
        name="ada",
    )(c_all, ada_w, ada_b.reshape(L, 1, N))


def _modnorm_kernel(x_ref, g_ref, sh_ref, sc_ref, h_ref):
    h = (_rms(x_ref[...]) * g_ref[...]) * (1.0 + sc_ref[...]) + sh_ref[...]
    h_ref[...] = h.astype(h_ref.dtype)


def _router_kernel(x_ref, g_ref, sh_ref, sc_ref, rw_ref, rb_ref, c0_ref, *rest, n_valid, aliased):
    h_ref, e_ref, p_ref, r_ref, cnt_ref, carry_ref = rest[1:] if aliased else rest

    @pl.when(pl.program_id(0) == 0)
    def _():
        carry_ref[...] = c0_ref[...]

    h = (_rms(x_ref[...]) * g_ref[...]) * (1.0 + sc_ref[...]) + sh_ref[...]
    hb = h.astype(BF16)
    half = h.shape[1] // 2
    lo = pltpu.bitcast(hb[:, :half].astype(F32), jnp.uint32)
    hi = pltpu.bitcast(hb[:, half:].astype(F32), jnp.uint32)
    h_ref[...] = hi | (lo >> 16)
    logits = jnp.dot(hb, rw_ref[...].astype(BF16), preferred_element_type=F32) + rb_ref[...]
    tm = logits.shape[0]
    lane = lax.broadcasted_iota(I32, logits.shape, 1).astype(F32)
    cur = logits
    vals, idxs = [], []
    for _ in range(TOP_K):
        m = jnp.max(cur, axis=-1, keepdims=True)
        idx = jnp.min(jnp.where(cur == m, lane, float(LANES)), axis=-1, keepdims=True)
        vals.append(m)
        idxs.append(idx)
        cur = jnp.where(lane == idx, -jnp.inf, cur)
    exps = [jnp.exp(v - vals[0]) for v in vals]
    denom = exps[0]
    for e in exps[1:]:
        denom = denom + e

    valid = lax.broadcasted_iota(I32, logits.shape, 0) < n_valid
    memb = jnp.zeros(logits.shape, F32)
    for k in range(TOP_K):
        memb = jnp.where((lane == idxs[k]) & valid, 1.0, memb)
    earlier = lax.broadcasted_iota(I32, (tm, tm), 1) < lax.broadcasted_iota(I32, (tm, tm), 0)
    before = jnp.dot(jnp.where(earlier, 1.0, 0.0).astype(BF16), memb.astype(BF16),
                     preferred_element_type=F32) + carry_ref[...]
    total = carry_ref[...] + jnp.sum(memb, axis=0, keepdims=True)
    carry_ref[...] = total
    cnt_ref[...] = total

    e_out = jnp.zeros(logits.shape, F32)
    p_out = jnp.zeros(logits.shape, F32)
    r_out = jnp.zeros(logits.shape, F32)
    for k in range(TOP_K):
        rank = jnp.sum(jnp.where(lane == idxs[k], before, 0.0), axis=-1, keepdims=True)
        e_out = jnp.where(lane == float(k), idxs[k], e_out)
        p_out = jnp.where(lane == float(k), exps[k] / denom, p_out)
        r_out = jnp.where(lane == float(k), rank, r_out)
    e_ref[...] = e_out.astype(I32)
    p_ref[...] = p_out
    r_ref[...] = r_out.astype(I32)


def _mod_block(tm, D, rows_per_mod):
    if rows_per_mod == 1:
        return (tm, D), lambda i: (i, 0)
    return (None, 1, D), lambda i: ((i * tm) // rows_per_mod, 0, 0)


def _mod_specs(n_rows, tm, D, rows_per_mod):
    shape, index = _mod_block(tm, D, rows_per_mod)
    return pl.BlockSpec(shape, index)


def _modnorm(x, g, shift, scale, rows_per_mod, out_dtype):
    M, D = x.shape
    tm = _row_tile(M, 256, rows_per_mod)
    mod = _mod_specs(M, tm, D, rows_per_mod)
    return pl.pallas_call(
        _modnorm_kernel,
        grid=(M // tm,),
        in_specs=[pl.BlockSpec((tm, D), lambda i: (i, 0)), pl.BlockSpec((1, D), lambda i: (0, 0)), mod, mod],
        out_specs=pl.BlockSpec((tm, D), lambda i: (i, 0)),
        out_shape=jax.ShapeDtypeStruct((M, D), out_dtype),
        compiler_params=_params(("arbitrary",), 40),
        name="modnorm",
    )(x, g.reshape(1, D), shift, scale)


def _modnorm_router(x, g, shift, scale, rows_per_mod, rw_pad, rb_pad, counts0, n_valid, total_rows, h_all=None):
    M, D = x.shape
    tm = _row_tile(M, 256, rows_per_mod)
    mod = _mod_specs(M, tm, D, rows_per_mod)
    row_off = 0 if h_all is None else total_rows - M
    assert row_off % tm == 0
    boff = row_off // tm
    small = pl.BlockSpec((tm, LANES), lambda i: (i, 0))
    in_specs = [pl.BlockSpec((tm, D), lambda i: (i, 0)), pl.BlockSpec((1, D), lambda i: (0, 0)), mod, mod,
                pl.BlockSpec((D, LANES), lambda i: (0, 0)), pl.BlockSpec((1, LANES), lambda i: (0, 0)),
                pl.BlockSpec((1, LANES), lambda i: (0, 0))]
    args = [x, g.reshape(1, D), shift, scale, rw_pad, rb_pad, counts0]
    aliases = {}
    if h_all is not None:
        in_specs.append(pl.BlockSpec(memory_space=pl.ANY))
        args.append(h_all)
        aliases = {len(args) - 1: 0}
    return pl.pallas_call(
        functools.partial(_router_kernel, n_valid=n_valid, aliased=h_all is not None),
        grid=(M // tm,),
        in_specs=in_specs,
        out_specs=[pl.BlockSpec((tm, D // 2), lambda i: (boff + i, 0)), small, small, small,
                   pl.BlockSpec((1, LANES), lambda i: (0, 0))],
        out_shape=[jax.ShapeDtypeStruct((total_rows, D // 2), jnp.uint32), jax.ShapeDtypeStruct((M, LANES), I32),
                   jax.ShapeDtypeStruct((M, LANES), F32), jax.ShapeDtypeStruct((M, LANES), I32),
                   jax.ShapeDtypeStruct((1, LANES), F32)],
        scratch_shapes=[pltpu.VMEM((1, LANES), F32)],
        input_output_aliases=aliases,
        compiler_params=_params(("arbitrary",), 48),
        name="modnorm_router",
    )(*args)


def _rmsnorm_kernel(x_ref, g_ref, o_ref):
    o_ref[...] = _rms(x_ref[...]) * g_ref[...]


def _rmsnorm(x, g):
    M, D = x.shape
    tm = _tile(M, 256)
    return pl.pallas_call(
        _rmsnorm_kernel,
        grid=(M // tm,),
        in_specs=[pl.BlockSpec((tm, D), lambda i: (i, 0)), pl.BlockSpec((1, D), lambda i: (0, 0))],
        out_specs=pl.BlockSpec((tm, D), lambda i: (i, 0)),
        out_shape=jax.ShapeDtypeStruct((M, D), F32),
        compiler_params=_params(("arbitrary",), 40),
        name="final_rmsnorm",
    )(x, g.reshape(1, D))


def _mm_kernel(x_ref, w_ref, *rest, mode, lane_shift, aliased, staged):
    if lane_shift:
        wn_ref, rest = rest[0], rest[1:]
    if aliased:
        rest = rest[1:]

    if staged:
        wb_ref = rest[-1]

        @pl.when(pl.program_id(1) == 0)
        def _():
            if not lane_shift:
                wb_ref[...] = w_ref[...].astype(BF16)
            else:
                K, tn = w_ref.shape
                kc = _tile(K, 512)
                for kk in range(K // kc):
                    rows = slice(kk * kc, (kk + 1) * kc)
                    wide = jnp.concatenate([w_ref[rows, :].astype(F32), wn_ref[rows, :].astype(F32)], axis=1)
                    wb_ref[rows, :] = pltpu.roll(wide, tn + LANES - lane_shift, axis=1)[:, :tn].astype(BF16)

        w = wb_ref[...]
    else:
        w = w_ref[...]
    acc = jnp.dot(x_ref[...], w, preferred_element_type=F32)
    if mode == "f32":
        rest[0][...] = acc
    elif mode == "bf16":
        rest[0][...] = acc.astype(BF16)
    elif mode == "both":
        rest[0][...] = acc
        rest[1][...] = acc.astype(BF16)
    elif mode == "logf":
        fb_ref, o_ref = rest[0], rest[1]
        o_ref[...] = jax.nn.log_sigmoid(acc + fb_ref[...])
    elif mode == "resid":
        xr_ref, gate_ref, o_ref = rest[0], rest[1], rest[2]
        o_ref[...] = xr_ref[...] + gate_ref[...] * acc
    else:
        raise ValueError(mode)


def _matmul(x, w, w_lead, col_off, n_cols, mode, extras=(), rows_per_mod=None, tn_pref=512, lane_shift=0, stack=None):
    M, K = x.shape
    tm = _row_tile(M, 1024, rows_per_mod)
    tn = next((t for t in (tn_pref, 256, LANES) if n_cols % t == 0 and col_off % t == 0), n_cols)
    assert col_off % tn == 0
    joff = col_off // tn
    ni, nj = M // tm, n_cols // tn
    lead = tuple(w_lead)
    w_spec = pl.BlockSpec((None,) * len(lead) + (K, tn), lambda j, i: lead + (0, joff + j))
    in_specs = [pl.BlockSpec((tm, K), lambda j, i: (i, 0)), w_spec]
    if lane_shift:
        assert 0 < lane_shift < LANES and tn % LANES == 0
        per = tn // LANES
        in_specs.append(pl.BlockSpec((None,) * len(lead) + (K, LANES), lambda j, i: lead + (0, (joff + j + 1) * per)))
        extras = (w,) + tuple(extras)
    staged = bool(lane_shift) or w.dtype != BF16
    o_spec = pl.BlockSpec((tm, tn), lambda j, i: (i, j))
    aliases = {}
    if mode == "f32":
        out_specs, out_shape = [o_spec], [jax.ShapeDtypeStruct((M, n_cols), F32)]
    elif mode == "bf16":
        out_specs, out_shape = [o_spec], [jax.ShapeDtypeStruct((M, n_cols), BF16)]
    elif mode == "both":
        out_specs = [o_spec, o_spec]
        out_shape = [jax.ShapeDtypeStruct((M, n_cols), F32), jax.ShapeDtypeStruct((M, n_cols), BF16)]
        if stack is not None:
            buf, slab, n_slabs = stack
            out_specs[0] = pl.BlockSpec((None, tm, tn), lambda j, i: (slab, i, j))
            out_shape[0] = jax.ShapeDtypeStruct((n_slabs, M, n_cols), F32)
            if buf is not None:
                in_specs.append(pl.BlockSpec(memory_space=pl.ANY))
                extras = tuple(extras) + (buf,)
                aliases = {len(in_specs) - 1: 0}
    elif mode == "logf":
        in_specs.append(pl.BlockSpec((1, tn), lambda j, i: (0, j)))
        out_specs, out_shape = [o_spec], [jax.ShapeDtypeStruct((M, n_cols), F32)]
    elif mode == "resid":
        in_specs.append(o_spec)
        if rows_per_mod == 1:
            in_specs.append(pl.BlockSpec((tm, tn), lambda j, i: (i, j)))
        else:
            in_specs.append(pl.BlockSpec((None, 1, tn), lambda j, i: ((i * tm) // rows_per_mod, 0, j)))
        out_specs, out_shape = [o_spec], [jax.ShapeDtypeStruct((M, n_cols), F32)]
    outs = pl.pallas_call(
        functools.partial(_mm_kernel, mode=mode, lane_shift=lane_shift, aliased=bool(aliases), staged=staged),
        grid=(nj, ni),
        in_specs=in_specs,
        out_specs=out_specs,
        out_shape=out_shape,
        scratch_shapes=[pltpu.VMEM((K, tn), BF16)] if staged else [],
        input_output_aliases=aliases,
        compiler_params=_params(("arbitrary", "arbitrary"), 56),
        name="matmul_" + mode,
    )(x, w, *extras)
    return outs if len(outs) > 1 else outs[0]


def _cumsum_kernel(lf_ref, o_ref, *, H):
    x = lf_ref[...].T[:H]
    T = x.shape[1]
    lane = lax.broadcasted_iota(I32, x.shape, 1)
    d = 1
    while d < T:
        x = x + jnp.where(lane >= d, pltpu.roll(x, d, axis=1), 0.0)
        d *= 2
    o_ref[...] = x


def _cumsum_t(logf_pad, B, T, H):
    return pl.pallas_call(
        functools.partial(_cumsum_kernel, H=H),
        grid=(B,),
        in_specs=[pl.BlockSpec((T, LANES), lambda b: (b, 0))],
        out_specs=pl.BlockSpec((None, H, T), lambda b: (b, 0, 0)),
        out_shape=jax.ShapeDtypeStruct((B, H, T), F32),
        compiler_params=_params(("arbitrary",), 32),
        name="logf_cumsum",
    )(logf_pad)


def _fox_kernel(q_ref, k_ref, v_ref, c_ref, o_ref, *, tq, nq, scale):
    Dh = q_ref.shape[1]

    def run(n_blocks):
        q = q_ref[...]
        m = jnp.full((tq, 1), -jnp.inf, F32)
        l = jnp.zeros((tq, 1), F32)
        acc = jnp.zeros((tq, Dh), F32)
        for j in range(n_blocks):
            cols = slice(j * tq, (j + 1) * tq)
            s = lax.dot_general(q, k_ref[cols, :], (((1,), (1,)), ((), ())), preferred_element_type=F32) * scale
            s = s - c_ref[:, cols]
            if j == n_blocks - 1:
                row = lax.broadcasted_iota(I32, s.shape, 0)
                col = lax.broadcasted_iota(I32, s.shape, 1)
                s = jnp.where(col <= row, s, -jnp.inf)
            m_new = jnp.maximum(m, jnp.max(s, axis=-1, keepdims=True))
            alpha = jnp.exp(m - m_new)
            p = jnp.exp(s - m_new)
            l = alpha * l + jnp.sum(p, axis=-1, keepdims=True)
            acc = alpha * acc + jnp.dot(p.astype(BF16), v_ref[cols, :], preferred_element_type=F32)
            m = m_new
        o_ref[...] = acc / l

    qi = pl.program_id(2)
    for qv in range(nq):
        pl.when(qi == qv)(functools.partial(run, qv + 1))


def _fox_prompt(q, k, v, ct, B, T, H, Dh):
    tq = _tile(T, 512)
    nq = T // tq
    return pl.pallas_call(
        functools.partial(_fox_kernel, tq=tq, nq=nq, scale=Dh ** -0.5),
        grid=(B, H, nq),
        in_specs=[
            pl.BlockSpec((tq, Dh), lambda b, h, i: (b * nq + i, h)),
            pl.BlockSpec((T, Dh), lambda b, h, i: (b, h)),
            pl.BlockSpec((T, Dh), lambda b, h, i: (b, h)),
            pl.BlockSpec((None, 1, T), lambda b, h, i: (b * H + h, 0, 0)),
        ],
        out_specs=pl.BlockSpec((tq, Dh), lambda b, h, i: (b * nq + i, h)),
        out_shape=jax.ShapeDtypeStruct((B * T, H * Dh), F32),
        compiler_params=_params(("arbitrary", "arbitrary", "arbitrary"), 32),
        name="fox_prompt",
    )(q, k, v, ct.reshape(B * H, 1, T))


def _seg_prefix(x, stride):
    W = x.shape[1]
    lane = lax.broadcasted_iota(I32, x.shape, 1)
    d = stride
    while d < W:
        x = x + jnp.where(lane >= d, pltpu.roll(x, d, axis=1), 0.0)
        d *= 2
    return x


def _seg_allreduce(x, stride, op):
    W = x.shape[1]
    d = stride
    while d < W:
        x = op(x, pltpu.roll(x, d, axis=1))
        d *= 2
    return x


def _decode_kernel(pt_ref, q_ref, kn_ref, vn_ref, lfn_ref, *rest, NP, PS, H, G, scale):
    k_refs, v_refs, f_refs = rest[:G], rest[G:2 * G], rest[2 * G:3 * G]
    o_ref = rest[3 * G]
    s_ref, lf_ref, p_ref, acc_ref, pn_ref = rest[3 * G + 1:]
    ph = pl.program_id(1)
    pg = pl.program_id(2)
    W = PS * H
    sub = lax.broadcasted_iota(I32, (H, W), 0)
    lane = lax.broadcasted_iota(I32, (H, W), 1)
    own = (lane % H) == sub

    @pl.when(ph == 0)
    def _():
        qb = q_ref[...].astype(BF16)
        for g in range(G):
            kb = k_refs[g][...].reshape(W, -1).astype(BF16)
            s = lax.dot_general(qb, kb, (((1,), (1,)), ((), ())), preferred_element_type=F32)
            row = pg * G + g
            s_ref[pl.ds(row, 1), :] = jnp.sum(jnp.where(own, s, 0.0), axis=0, keepdims=True)
            lf_ref[pl.ds(row, 1), :] = f_refs[g][...]

    @pl.when(ph == 1)
    def _():
        @pl.when(pg == 0)
        def _():
            lf = lf_ref[...]
            within = _seg_prefix(lf, H)
            page_tot = _seg_allreduce(lf, H, jnp.add)
            later = (lax.broadcasted_iota(I32, (NP, NP), 1) > lax.broadcasted_iota(I32, (NP, NP), 0)).astype(F32)
            after = jnp.dot(later, page_tot, precision=lax.Precision.HIGHEST, preferred_element_type=F32)
            logits = s_ref[...] * scale + (lfn_ref[...] + after + (page_tot - within))
            q_r = q_ref[...].astype(BF16).astype(F32)
            kn_r = kn_ref[...].astype(BF16).astype(F32)
            s_new_col = jnp.sum(q_r * kn_r, axis=-1, keepdims=True) * scale
            s_new = jnp.sum(jnp.where(own, s_new_col, 0.0), axis=0, keepdims=True)
            m = _seg_allreduce(jnp.max(logits, axis=0, keepdims=True), H, jnp.maximum)
            m = jnp.maximum(m, s_new)
            e = jnp.exp(logits - m)
            e_new = jnp.exp(s_new - m)
            denom = _seg_allreduce(jnp.sum(e, axis=0, keepdims=True), H, jnp.add) + e_new
            p_ref[...] = e / denom
            pn_ref[...] = jnp.sum(jnp.where(lane == sub, e_new / denom, 0.0), axis=1, keepdims=True)
            acc_ref[...] = jnp.zeros_like(acc_ref)

        acc = acc_ref[...]
        for g in range(G):
            row = pg * G + g
            pe = jnp.where(own, jnp.broadcast_to(p_ref[pl.ds(row, 1), :], (H, W)), 0.0).astype(BF16)
            vb = v_refs[g][...].reshape(W, -1).astype(BF16)
            acc = acc + jnp.dot(pe, vb, preferred_element_type=F32)
        acc_ref[...] = acc

        @pl.when(pg == NP // G - 1)
        def _():
            o_ref[...] = acc + pn_ref[...].astype(BF16).astype(F32) * vn_ref[...].astype(BF16).astype(F32)


def _fox_decode(q, kn, vn, lf_new, cache_k, cache_v, cache_logf, page_table, l):
    Bs, H, Dh = q.shape
    NP = page_table.shape[1]
    PS = cache_k.shape[2]
    W = PS * H
    assert PS & (PS - 1) == 0, "the strided lane scans need a power-of-two page size"
    G = next(g for g in (8, 4, 2, 1) if NP % g == 0)
    lf_pages = cache_logf.reshape(cache_logf.shape[0], cache_logf.shape[1], 1, W)
    lfn = jnp.tile(lf_new, (1, PS)).reshape(Bs, 1, W)

    def page(b, ph, pg, pt, g, frozen_phase, frozen_at):
        return pt[b * NP + jnp.where(ph == frozen_phase, frozen_at + g, pg * G + g)]

    tok = lambda b, ph, pg, pt: (b, 0, 0)
    k_specs = [pl.BlockSpec((None, None, PS, H, Dh),
                            lambda b, ph, pg, pt, g=g: (l, page(b, ph, pg, pt, g, 1, NP - G), 0, 0, 0)) for g in range(G)]
    v_specs = [pl.BlockSpec((None, None, PS, H, Dh),
                            lambda b, ph, pg, pt, g=g: (l, page(b, ph, pg, pt, g, 0, 0), 0, 0, 0)) for g in range(G)]
    f_specs = [pl.BlockSpec((None, None, 1, W),
                            lambda b, ph, pg, pt, g=g: (l, page(b, ph, pg, pt, g, 1, NP - G), 0, 0)) for g in range(G)]
    grid_spec = pltpu.PrefetchScalarGridSpec(
        num_scalar_prefetch=1,
        grid=(Bs, 2, NP // G),
        in_specs=[pl.BlockSpec((None, H, Dh), tok), pl.BlockSpec((None, H, Dh), tok), pl.BlockSpec((None, H, Dh), tok),
                  pl.BlockSpec((None, 1, W), tok)] + k_specs + v_specs + f_specs,
        out_specs=pl.BlockSpec((None, H, Dh), tok),
        scratch_shapes=[pltpu.VMEM((NP, W), F32), pltpu.VMEM((NP, W), F32), pltpu.VMEM((NP, W), F32),
                        pltpu.VMEM((H, Dh), F32), pltpu.VMEM((H, 1), F32)],
    )
    return pl.pallas_call(
        functools.partial(_decode_kernel, NP=NP, PS=PS, H=H, G=G, scale=Dh ** -0.5),
        grid_spec=grid_spec,
        out_shape=jax.ShapeDtypeStruct((Bs, H, Dh), F32),
        compiler_params=_params(("arbitrary", "arbitrary", "arbitrary"), 48),
        name="fox_decode",
    )(page_table.reshape(-1), q, kn, vn, lfn, *([cache_k] * G), *([cache_v] * G), *([lf_pages] * G))


def _gmlp_norm_v(zv, g, b):
    gv = _gelu(zv)
    mu = jnp.mean(gv, axis=-1, keepdims=True)
    var = jnp.mean(jnp.square(gv - mu), axis=-1, keepdims=True)
    return (gv - mu) * lax.rsqrt(var + NORM_EPS) * g + b


def _gmlp_kernel(zu_ref, zv_ref, g_ref, b_ref, ws_ref, bst_ref, o_ref, *, Hg, CW):
    v = _gmlp_norm_v(zv_ref[...], g_ref[...], b_ref[...])
    C = v.shape[0]
    tril = lax.broadcasted_iota(I32, (C, C), 0) >= lax.broadcasted_iota(I32, (C, C), 1)
    for h in range(Hg):
        sl = slice(h * CW, (h + 1) * CW)
        w = jnp.where(tril, ws_ref[h], 0.0).astype(BF16)
        s = jnp.dot(w, v[:, sl].astype(BF16), preferred_element_type=F32) + bst_ref[:, h:h + 1]
        o_ref[:, sl] = _gelu(zu_ref[:, sl]) * s


def _gmlp_prompt(zg, ln_g, ln_b, ws, bs):
    M, DG2 = zg.shape
    DG = DG2 // 2
    Hg, C, _ = ws.shape
    CW = DG // Hg
    return pl.pallas_call(
        functools.partial(_gmlp_kernel, Hg=Hg, CW=CW),
        grid=(M // C,),
        in_specs=[
            pl.BlockSpec((C, DG), lambda i: (i, 0)), pl.BlockSpec((C, DG), lambda i: (i, 1)),
            pl.BlockSpec((1, DG), lambda i: (0, 0)), pl.BlockSpec((1, DG), lambda i: (0, 0)),
            pl.BlockSpec((Hg, C, C), lambda i: (0, 0, 0)), pl.BlockSpec((C, Hg), lambda i: (0, 0)),
        ],
        out_specs=pl.BlockSpec((C, DG), lambda i: (i, 0)),
        out_shape=jax.ShapeDtypeStruct((M, DG), F32),
        compiler_params=_params(("arbitrary",), 32),
        name="gmlp_prompt",
    )(zg, zg, ln_g.reshape(1, DG), ln_b.reshape(1, DG), ws, bs.T)


def _gmlp_sample_kernel(zu_ref, zv_ref, g_ref, b_ref, w0_ref, b0_ref, v_ref, o_ref):
    v = _gmlp_norm_v(zv_ref[...], g_ref[...], b_ref[...])
    v_ref[...] = v
    mix = w0_ref[...].astype(BF16).astype(F32) * v.astype(BF16).astype(F32)
    o_ref[...] = _gelu(zu_ref[...]) * (mix + b0_ref[...])


def _gmlp_sample(zg, ln_g, ln_b, ws, bs):
    R, DG2 = zg.shape
    DG = DG2 // 2
    Hg = ws.shape[0]
    CW = DG // Hg
    w0 = jnp.repeat(ws[:, 0, 0], CW).reshape(1, DG)
    b0 = jnp.repeat(bs[:, 0], CW).reshape(1, DG)
    row = pl.BlockSpec((1, DG), lambda i: (0, 0))
    return pl.pallas_call(
        _gmlp_sample_kernel,
        grid=(1,),
        in_specs=[pl.BlockSpec((R, DG), lambda i: (0, 0)), pl.BlockSpec((R, DG), lambda i: (0, 1)), row, row, row, row],
        out_specs=[pl.BlockSpec((R, DG), lambda i: (0, 0)), pl.BlockSpec((R, DG), lambda i: (0, 0))],
        out_shape=[jax.ShapeDtypeStruct((R, DG), F32), jax.ShapeDtypeStruct((R, DG), F32)],
        name="gmlp_sample",
    )(zg, zg, ln_g.reshape(1, DG), ln_b.reshape(1, DG), w0, b0)


def _merge_kernel(a_ref, b_ref, ga_ref, gb_ref, o_ref):
    DA = a_ref.shape[1]
    o_ref[:, :DA] = (_rms(a_ref[...]) * ga_ref[...]).astype(BF16)
    o_ref[:, DA:] = (_rms(b_ref[...]) * gb_ref[...]).astype(BF16)


def _merge_norm(a, b, g):
    M, DA = a.shape
    DB = b.shape[1]
    tm = _tile(M, 256)
    return pl.pallas_call(
        _merge_kernel,
        grid=(M // tm,),
        in_specs=[pl.BlockSpec((tm, DA), lambda i: (i, 0)), pl.BlockSpec((tm, DB), lambda i: (i, 0)),
                  pl.BlockSpec((1, DA), lambda i: (0, 0)), pl.BlockSpec((1, DB), lambda i: (0, 0))],
        out_specs=pl.BlockSpec((tm, DA + DB), lambda i: (i, 0)),
        out_shape=jax.ShapeDtypeStruct((M, DA + DB), BF16),
        compiler_params=_params(("arbitrary",), 32),
        name="merge_norm",
    )(a, b, g[:DA].reshape(1, DA), g[DA:].reshape(1, DB))


def _dispatch_kernel(nbu_ref, dest_ref, h_ref, o_ref, tok_ref, buf_ref, sem, *, tm, n_assign, n_rows, sentinel):
    b = pl.program_id(0)
    nbu = nbu_ref[0]

    def issue(blk, slot):
        def body(r, c):
            t = tok_ref[blk * tm + r]
            pltpu.make_async_copy(h_ref.at[pl.ds(t, 1)], buf_ref.at[slot, pl.ds(r, 1)], sem.at[slot]).start()
            return c

        lax.fori_loop(0, tm, body, 0, unroll=8)

    @pl.when(b == 0)
    def _():
        group = 16
        assert group % TOP_K == 0

        def fill(i, c):
            for u in range(group):
                tok_ref[i * group + u] = sentinel
            return c

        lax.fori_loop(0, n_rows // group, fill, 0)

        def scatter(i, c):
            base = i * group
            rows = [dest_ref[base + u] for u in range(group)]
            for u in range(group):
                tok_ref[rows[u]] = i * (group // TOP_K) + u // TOP_K
            return c

        lax.fori_loop(0, n_assign // group, scatter, 0)

        def scatter_tail(a, c):
            tok_ref[dest_ref[a]] = lax.div(a, TOP_K)
            return c

        lax.fori_loop(n_assign // group * group, n_assign, scatter_tail, 0)
        issue(0, 0)

    slot = b % 2

    @pl.when(b < nbu)
    def _():
        @pl.when(b + 1 < nbu)
        def _():
            issue(b + 1, 1 - slot)

        pltpu.make_async_copy(h_ref.at[pl.ds(0, tm)], buf_ref.at[slot], sem.at[slot]).wait()
        words = buf_ref[slot]
        half = words.shape[1]
        o_ref[:, :half] = pltpu.bitcast(words << 16, F32).astype(BF16)
        o_ref[:, half:] = pltpu.bitcast(words & jnp.uint32(0xFFFF0000), F32).astype(BF16)

    @pl.when(b >= nbu)
    def _():
        o_ref[...] = jnp.zeros_like(o_ref)


def _dispatch(nbu, dest, h_all, tm, nb, sentinel):
    D = 2 * h_all.shape[1]
    n_assign = dest.shape[0]
    grid_spec = pltpu.PrefetchScalarGridSpec(
        num_scalar_prefetch=2,
        grid=(nb,),
        in_specs=[pl.BlockSpec(memory_space=pl.ANY)],
        out_specs=pl.BlockSpec((tm, D), lambda b, nbu, dest: (b, 0)),
        scratch_shapes=[pltpu.SMEM((nb * tm,), I32), pltpu.VMEM((2, tm, D // 2), jnp.uint32),
                        pltpu.SemaphoreType.DMA((2,))],
    )
    return pl.pallas_call(
        functools.partial(_dispatch_kernel, tm=tm, n_assign=n_assign, n_rows=nb * tm, sentinel=sentinel),
        grid_spec=grid_spec,
        out_shape=jax.ShapeDtypeStruct((nb * tm, D), BF16),
        compiler_params=_params(("arbitrary",), 32),
        name="moe_dispatch",
    )(nbu, dest, h_all)


def _expert_rows_loop(n, s0, n_next, s0_next, is_first, tm, x_hbm, xbuf, sin, out_hbm, obuf, sout, col0, tn, compute):
    def x_copy(blk, slot):
        return pltpu.make_async_copy(x_hbm.at[pl.ds(blk * tm, tm)], xbuf.at[slot], sin.at[slot])

    def o_copy(i, slot):
        return pltpu.make_async_copy(obuf.at[slot], out_hbm.at[pl.ds((s0 + i) * tm, tm), pl.ds(col0, tn)], sout.at[slot])

    for first in range(2):
        @pl.when(is_first & (n > first))
        def _():
            x_copy(s0 + first, first).start()

    def body(i, c):
        @pl.when(i + 2 < n)
        def _():
            x_copy(s0 + i + 2, (i + 2) % 3).start()

        x_copy(s0 + i, i % 3).wait()
        oslot = i % 2

        @pl.when(i >= 2)
        def _():
            o_copy(i - 2, oslot).wait()

        obuf[oslot] = compute(xbuf[i % 3])
        o_copy(i, oslot).start()
        return c

    lax.fori_loop(0, n, body, 0)

    for first in range(2):
        @pl.when(n_next > first)
        def _():
            x_copy(s0_next + first, first).start()

    @pl.when(n >= 2)
    def _():
        o_copy(n - 2, n % 2).wait()

    @pl.when(n >= 1)
    def _():
        o_copy(n - 1, (n - 1) % 2).wait()


def _next_expert(cnt_ref, start_ref, j, e):
    n_e = pl.num_programs(1)
    last = (j == pl.num_programs(0) - 1) & (e == n_e - 1)
    e_next = jnp.where(e == n_e - 1, 0, e + 1)
    return jnp.where(last, 0, cnt_ref[e_next]), start_ref[e_next], (j == 0) & (e == 0)


def _zero_tail(nbu, nb, tm, out_hbm, obuf, sout, col0, tn):
    obuf[0] = jnp.zeros(obuf.shape[1:], obuf.dtype)

    def body(blk, c):
        cp = pltpu.make_async_copy(obuf.at[0], out_hbm.at[pl.ds(blk * tm, tm), pl.ds(col0, tn)], sout.at[0])
        cp.start()
        cp.wait()
        return c

    lax.fori_loop(nbu, nb, body, 0)


def _gate_up_kernel(start_ref, cnt_ref, nbu_ref, x_hbm, wg_ref, wu_ref, bg_ref, bu_ref, a_hbm,
                    wgb_ref, wub_ref, xbuf, obuf, sin, sout, *, tm, tn, nb):
    j = pl.program_id(0)
    e = pl.program_id(1)
    n = cnt_ref[e]

    def compute(x):
        g = jnp.dot(x, wgb_ref[...], preferred_element_type=F32) + bg_ref[...]
        u = jnp.dot(x, wub_ref[...], preferred_element_type=F32) + bu_ref[...]
        g = jnp.minimum(g, SWIGLU_LIMIT)
        u = jnp.clip(u, -SWIGLU_LIMIT, SWIGLU_LIMIT)
        return (g * jax.nn.sigmoid(SWIGLU_ALPHA * g) * (u + 1.0)).astype(BF16)

    @pl.when(n > 0)
    def _():
        wgb_ref[...] = wg_ref[...].astype(BF16)
        wub_ref[...] = wu_ref[...].astype(BF16)

    n_next, s_next, is_first = _next_expert(cnt_ref, start_ref, j, e)
    _expert_rows_loop(n, start_ref[e], n_next, s_next, is_first, tm, x_hbm, xbuf, sin, a_hbm, obuf, sout,
                      j * tn, tn, compute)

    @pl.when(e == pl.num_programs(1) - 1)
    def _():
        _zero_tail(nbu_ref[0], nb, tm, a_hbm, obuf, sout, j * tn, tn)


def _gate_up(blk_start, blk_cnt, nbu, xs, w_gate_up, b_gate_up, l, tm):
    R, D = xs.shape
    _, E, _, DE2 = w_gate_up.shape
    DE = DE2 // 2
    tn = _tile(DE, 512)
    nj = DE // tn
    grid_spec = pltpu.PrefetchScalarGridSpec(
        num_scalar_prefetch=3,
        grid=(nj, E),
        in_specs=[
            pl.BlockSpec(memory_space=pl.ANY),
            pl.BlockSpec((None, None, D, tn), lambda j, e, st, ct, nu: (l, e, 0, j)),
            pl.BlockSpec((None, None, D, tn), lambda j, e, st, ct, nu: (l, e, 0, nj + j)),
            pl.BlockSpec((None, None, 1, tn), lambda j, e, st, ct, nu: (l, e, 0, j)),
            pl.BlockSpec((None, None, 1, tn), lambda j, e, st, ct, nu: (l, e, 0, nj + j)),
        ],
        out_specs=pl.BlockSpec(memory_space=pl.ANY),
        scratch_shapes=[pltpu.VMEM((D, tn), BF16), pltpu.VMEM((D, tn), BF16), pltpu.VMEM((3, tm, D), BF16),
                        pltpu.VMEM((2, tm, tn), BF16), pltpu.SemaphoreType.DMA((3,)), pltpu.SemaphoreType.DMA((2,))],
    )
    bgu = b_gate_up.reshape(b_gate_up.shape[0], E, 1, DE2)
    return pl.pallas_call(
        functools.partial(_gate_up_kernel, tm=tm, tn=tn, nb=R // tm),
        grid_spec=grid_spec,
        out_shape=jax.ShapeDtypeStruct((R, DE), BF16),
        compiler_params=_params(("arbitrary", "arbitrary"), 56),
        name="moe_gate_up",
    )(blk_start, blk_cnt, nbu, xs, w_gate_up, w_gate_up, bgu, bgu)


def _down_kernel(start_ref, cnt_ref, nbu_ref, a_hbm, w_ref, bias_ref, o_hbm, wb_ref, xbuf, obuf, sin, sout, *, tm, tn, nb):
    j = pl.program_id(0)
    e = pl.program_id(1)
    n = cnt_ref[e]

    def compute(a):
        return jnp.dot(a, wb_ref[...], preferred_element_type=F32) + bias_ref[...]

    @pl.when(n > 0)
    def _():
        wb_ref[...] = w_ref[...].astype(BF16)

    n_next, s_next, is_first = _next_expert(cnt_ref, start_ref, j, e)
    _expert_rows_loop(n, start_ref[e], n_next, s_next, is_first, tm, a_hbm, xbuf, sin, o_hbm, obuf, sout,
                      j * tn, tn, compute)

    @pl.when(e == pl.num_programs(1) - 1)
    def _():
        _zero_tail(nbu_ref[0], nb, tm, o_hbm, obuf, sout, j * tn, tn)


def _down(blk_start, blk_cnt, nbu, a, w_down, b_down, l, tm):
    R, DE = a.shape
    _, E, _, D = w_down.shape
    tn = _tile(D, 2048)
    nj = D // tn
    grid_spec = pltpu.PrefetchScalarGridSpec(
        num_scalar_prefetch=3,
        grid=(nj, E),
        in_specs=[
            pl.BlockSpec(memory_space=pl.ANY),
            pl.BlockSpec((None, None, DE, tn), lambda j, e, st, ct, nu: (l, e, 0, j)),
            pl.BlockSpec((None, None, 1, tn), lambda j, e, st, ct, nu: (l, e, 0, j)),
        ],
        out_specs=pl.BlockSpec(memory_space=pl.ANY),
        scratch_shapes=[pltpu.VMEM((DE, tn), BF16), pltpu.VMEM((3, tm, DE), BF16), pltpu.VMEM((2, tm, tn), F32),
                        pltpu.SemaphoreType.DMA((3,)), pltpu.SemaphoreType.DMA((2,))],
    )
    return pl.pallas_call(
        functools.partial(_down_kernel, tm=tm, tn=tn, nb=R // tm),
        grid_spec=grid_spec,
        out_shape=jax.ShapeDtypeStruct((R, D), F32),
        compiler_params=_params(("arbitrary", "arbitrary"), 56),
        name="moe_down",
    )(blk_start, blk_cnt, nbu, a, w_down, b_down.reshape(b_down.shape[0], E, 1, D))


def _combine_kernel(pos_ref, rows_ref, p_ref, x_ref, gate_ref, o_ref, buf_ref, sem, *, tc):
    i = pl.program_id(0)
    n = tc * TOP_K

    def issue(blk, slot):
        def body(r, c):
            for k in range(TOP_K):
                src = pos_ref[blk * n + r * TOP_K + k]
                pltpu.make_async_copy(rows_ref.at[pl.ds(src, 1)], buf_ref.at[slot, pl.ds(k * tc + r, 1)],
                                      sem.at[slot]).start()
            return c

        lax.fori_loop(0, tc, body, 0, unroll=4)

    @pl.when(i == 0)
    def _():
        issue(0, 0)

    slot = i % 2

    @pl.when(i + 1 < pl.num_programs(0))
    def _():
        issue(i + 1, 1 - slot)

    pltpu.make_async_copy(rows_ref.at[pl.ds(0, n)], buf_ref.at[slot], sem.at[slot]).wait()
    y = p_ref[:, 0:1] * buf_ref[slot, pl.ds(0, tc)]
    for k in range(1, TOP_K):
        y = y + p_ref[:, k:k + 1] * buf_ref[slot, pl.ds(k * tc, tc)]
    o_ref[...] = x_ref[...] + gate_ref[...] * y


def _combine(pos, rows, probs, x, gate, rows_per_mod):
    M, D = x.shape
    tc = _row_tile(M, 128, rows_per_mod)
    gate_shape, gate_index = _mod_block(tc, D, rows_per_mod)
    grid_spec = pltpu.PrefetchScalarGridSpec(
        num_scalar_prefetch=1,
        grid=(M // tc,),
        in_specs=[
            pl.BlockSpec(memory_space=pl.ANY),
            pl.BlockSpec((tc, LANES), lambda i, pos: (i, 0)),
            pl.BlockSpec((tc, D), lambda i, pos: (i, 0)),
            pl.BlockSpec(gate_shape, lambda i, pos: gate_index(i)),
        ],
        out_specs=pl.BlockSpec((tc, D), lambda i, pos: (i, 0)),
        scratch_shapes=[pltpu.VMEM((2, TOP_K * tc, D), F32), pltpu.SemaphoreType.DMA((2,))],
    )
    return pl.pallas_call(
        functools.partial(_combine_kernel, tc=tc),
        grid_spec=grid_spec,
        out_shape=jax.ShapeDtypeStruct((M, D), F32),
        compiler_params=_params(("arbitrary",), 48),
        name="moe_combine",
    )(pos, rows, probs, x, gate)


def _route(top_e, rank, counts, tm):
    E = counts.shape[0]
    n_assign = top_e.shape[0] * TOP_K
    padded = (counts + tm - 1) // tm * tm
    pad_end = jnp.cumsum(padded)
    pad_start = pad_end - padded
    flat_e = top_e.reshape(n_assign)
    start = jnp.sum(jnp.where(flat_e[:, None] == jnp.arange(E, dtype=I32)[None, :], pad_start[None, :], 0), axis=1)
    dest = (start + rank.reshape(n_assign)).astype(I32)
    nb = -(-n_assign // tm) + E
    nbu = (pad_end[-1] // tm).astype(I32).reshape(1)
    return dest, (pad_start // tm).astype(I32), (padded // tm).astype(I32), nbu, nb


def kernel(x_prompt, x_sample, c_prompt, c_sample, cache_k, cache_v, cache_logf, page_table, ada_w, ada_b, norm1_g, norm2_g, w_in, forget_b, gmlp_ln_g, gmlp_ln_b, gmlp_ws, gmlp_bs, out_g, w_out, router_w, router_b, w_gate_up, b_gate_up, w_down, b_down, final_g):
    B, T, D = x_prompt.shape
    Bs, Ss, _ = x_sample.shape
    assert Ss == 1, "the sample group carries one new token per sequence"
    L = ada_w.shape[0]
    H, Dh = cache_k.shape[3], cache_k.shape[4]
    DF = H * Dh
    DG = gmlp_ln_g.shape[1]
    E = router_w.shape[2]
    M = B * T
    SP = -(-Bs // ROW_PAD) * ROW_PAD
    assert (3 * DF) % LANES == 0 and H <= LANES and E <= LANES
    moe_tm = 256

    n_c = B + Bs
    RC = -(-n_c // ROW_PAD) * ROW_PAD
    c_all = jnp.concatenate([c_prompt, c_sample, jnp.zeros((RC - n_c, D), F32)], axis=0)
    mod = _ada(c_all, ada_w, ada_b)

    xp = x_prompt.reshape(M, D)
    xs = jnp.pad(x_sample.reshape(Bs, D), ((0, SP - Bs), (0, 0)))
    page_table = page_table.astype(I32)

    w_in = w_in.astype(BF16)
    k_all = v_all = None
    fp_l, ks_l, vs_l, fs_l, gs_l = [], [], [], [], []
    for l in range(L):
        mp = [m.reshape(B, 1, D) for m in jnp.split(mod[l, :B], 6, axis=-1)]
        ms = [jnp.pad(m, ((0, SP - Bs), (0, 0))) for m in jnp.split(mod[l, B:n_c], 6, axis=-1)]
        fb_pad = jnp.pad(forget_b[l], (0, LANES - H)).reshape(1, LANES)
        groups = (
            (xp, mp, T, True),
            (xs, ms, 1, False),
        )
        new_x = []
        for x, md, rpm, is_prompt in groups:
            sh1, sc1, g1 = md[0], md[1], md[2]
            h = _modnorm(x, norm1_g[l], sh1, sc1, rpm, BF16)
            if is_prompt:
                q_b = _matmul(h, w_in, (l,), 0, DF, "bf16")
                k_all, k_b = _matmul(h, w_in, (l,), DF, DF, "both", stack=(k_all, l, L))
                v_all, v_b = _matmul(h, w_in, (l,), 2 * DF, DF, "both", stack=(v_all, l, L))
            else:
                qkv = _matmul(h, w_in, (l,), 0, 3 * DF, "f32")
                q_f, k_f, v_f = qkv[:, :DF], qkv[:, DF:2 * DF], qkv[:, 2 * DF:]
            logf_pad = _matmul(h, w_in, (l,), 3 * DF, LANES, "logf", extras=(fb_pad,))
            zg = _matmul(h, w_in, (l,), 3 * DF, 2 * DG, "f32", lane_shift=H)
            if is_prompt:
                ct = _cumsum_t(logf_pad, B, T, H)
                o_fox = _fox_prompt(q_b, k_b, v_b, ct, B, T, H, Dh)
                o_g = _gmlp_prompt(zg, gmlp_ln_g[l], gmlp_ln_b[l], gmlp_ws[l], gmlp_bs[l])
                fp_l.append(logf_pad[:, :H].reshape(B, T, H))
            else:
                lf_s = logf_pad[:Bs, :H]
                o_att = _fox_decode(q_f[:Bs].reshape(Bs, H, Dh), k_f[:Bs].reshape(Bs, H, Dh),
                                    v_f[:Bs].reshape(Bs, H, Dh), lf_s,
                                    cache_k, cache_v, cache_logf, page_table, l)
                o_fox = jnp.pad(o_att.reshape(Bs, DF), ((0, SP - Bs), (0, 0)))
                gv, o_g = _gmlp_sample(zg, gmlp_ln_g[l], gmlp_ln_b[l], gmlp_ws[l], gmlp_bs[l])
                ks_l.append(k_f[:Bs].reshape(Bs, 1, H, Dh))
                vs_l.append(v_f[:Bs].reshape(Bs, 1, H, Dh))
                fs_l.append(lf_s.reshape(Bs, 1, H))
                gs_l.append(gv[:Bs].reshape(Bs, 1, DG))
            on = _merge_norm(o_fox, o_g, out_g[l])
            new_x.append(_matmul(on, w_out, (l,), 0, D, "resid", extras=(x, g1), rows_per_mod=rpm))
        xp, xs = new_x

        rw_pad = jnp.pad(router_w[l], ((0, 0), (0, LANES - E)))
        rb_pad = jnp.pad(router_b[l], (0, LANES - E), constant_values=NEG_BIG).reshape(1, LANES)
        zero_counts = jnp.zeros((1, LANES), F32)
        h_all, e_p, p_p, r_p, cnt_p = _modnorm_router(xp, norm2_g[l], mp[3], mp[4], T, rw_pad, rb_pad,
                                                      zero_counts, M, M + SP)
        h_all, e_s, p_s, r_s, cnt = _modnorm_router(xs, norm2_g[l], ms[3], ms[4], 1, rw_pad, rb_pad,
                                                    cnt_p, Bs, M + SP, h_all=h_all)
        top_e = jnp.concatenate([e_p[:, :TOP_K], e_s[:Bs, :TOP_K]], axis=0)
        rank = jnp.concatenate([r_p[:, :TOP_K], r_s[:Bs, :TOP_K]], axis=0)
        dest, blk_start, blk_cnt, nbu, nb = _route(top_e, rank, cnt[0, :E].astype(I32), moe_tm)
        xs_sorted = _dispatch(nbu, dest, h_all, moe_tm, nb, M + Bs)
        act = _gate_up(blk_start, blk_cnt, nbu, xs_sorted, w_gate_up, b_gate_up, l, moe_tm)
        rows = _down(blk_start, blk_cnt, nbu, act, w_down, b_down, l, moe_tm)
        pos_p = dest[:M * TOP_K]
        pos_s = jnp.pad(dest[M * TOP_K:], (0, (SP - Bs) * TOP_K))
        p_s = jnp.where(jnp.arange(SP)[:, None] < Bs, p_s, 0.0)
        xp = _combine(pos_p, rows, p_p, xp, mp[5], T)
        xs = _combine(pos_s, rows, p_s, xs, ms[5], 1)

    y_prompt = _rmsnorm(xp, final_g).reshape(B, T, D)
    y_sample = _rmsnorm(xs, final_g)[:Bs].reshape(Bs, 1, D)
    return (y_prompt, y_sample, k_all.reshape(L, B, T, H, Dh), v_all.reshape(L, B, T, H, Dh), jnp.stack(fp_l),
            jnp.stack(ks_l), jnp.stack(vs_l), jnp.stack(fs_l), jnp.stack(gs_l))
```
